```python
import jax, jax.numpy as jnp
from jax import lax
import numpy as np

D_MODEL = 2048
BATCH = 2
SEQ = 4096
DEPTH = 4
DEC_BATCH = 8
DEC_SEQ = 1
PAST_LEN = 16384
PAGE_SIZE = 128

HEAD_DIM = D_MODEL // 16
H_SB = 6
H_MOBA = 6
H_MEM = 4
SB_W = H_SB * HEAD_DIM
MOBA_W = H_MOBA * HEAD_DIM
MEM_W = H_MEM * HEAD_DIM
MIX_W = SB_W + MOBA_W + MEM_W
IN_COLS = 3 * SB_W + 3 * MOBA_W + MEM_W
D_FF = 4 * D_MODEL
MEM_LEN = 256
MOBA_BLOCK = 256
MOBA_TOPK = 3
SB_Q_BLOCK = 128
MOBA_Q_CHUNK = 32
LN_EPS = 1e-5
RMS_EPS = 1e-6
NEG_INF = -1e30
ATTN_SCALE = HEAD_DIM ** -0.5
DEEPNORM_ALPHA = (2 * DEPTH) ** 0.25
DEEPNORM_BETA = (8 * DEPTH) ** -0.25

kernel_name = 'hymba_sb_moba_mem_deepnorm_step'


def layer_norm(x, g, b):
    xf = x.astype(jnp.float32)
    mu = jnp.mean(xf, axis=-1, keepdims=True)
    var = jnp.mean(jnp.square(xf - mu), axis=-1, keepdims=True)
    return ((xf - mu) * lax.rsqrt(var + LN_EPS) * g + b).astype(x.dtype)


def head_rms_norm(o, g):
    b, t = o.shape[:2]
    of = o.astype(jnp.float32)
    of = of * lax.rsqrt(jnp.mean(of * of, axis=-1, keepdims=True) + RMS_EPS)
    return (of.reshape(b, t, -1) * g).astype(o.dtype)


def alibi_slopes(n):
    return jnp.exp2(-8.0 * jnp.arange(1, n + 1, dtype=jnp.float32) / n)


def sweep_queries(fn, q, q_pos, block):
    b, t, h, d = q.shape
    if t <= block or t % block:
        return fn(q, q_pos)
    n = t // block
    qb = q.reshape(b, n, block, h, d).swapaxes(0, 1)
    pb = q_pos.reshape(n, block)
    out = lax.map(lambda a: fn(a[0], a[1]), (qb, pb))
    return out.swapaxes(0, 1).reshape(b, t, h, out.shape[-1])


def stick_breaking_block(q, q_pos, k, v, k_pos):
    z = jnp.einsum('bqhd,bkhd->bhqk', q, k).astype(jnp.float32) * ATTN_SCALE
    past = (k_pos[None, :] < q_pos[:, None])[None, None]
    log_keep = jnp.where(past, jax.nn.log_sigmoid(-z), 0.0)
    between = lax.cumsum(log_keep, axis=3, reverse=True) - log_keep
    w = jnp.where(past, jnp.exp(jax.nn.log_sigmoid(z) + between), 0.0)
    return jnp.einsum('bhqk,bkhd->bqhd', w.astype(v.dtype), v)


def moba_block(q, q_pos, kb, vb, k_mean, slopes, n_top):
    b, tq, h, _ = q.shape
    nb = k_mean.shape[2]
    own = q_pos // MOBA_BLOCK
    gate = jnp.einsum('bqhd,bhnd->bhqn', q.astype(jnp.float32), k_mean)
    past_blk = (jnp.arange(nb)[None, :] < own[:, None])[None, None]
    gate = jnp.where(past_blk, gate, NEG_INF)
    _, top = lax.top_k(gate, n_top)
    sel_ok = jnp.arange(n_top)[None, :] < jnp.minimum(own, n_top)[:, None]
    own_idx = jnp.broadcast_to(own[None, None, :, None], (b, h, tq, 1)).astype(top.dtype)
    idx = jnp.concatenate([top, own_idx], axis=-1)
    ok = jnp.concatenate([sel_ok, jnp.ones((tq, 1), bool)], axis=-1)
    gather = jax.vmap(jax.vmap(lambda blocks, i: blocks[i]))
    kg = gather(kb, idx)
    vg = gather(vb, idx)
    kpos = idx[..., None] * MOBA_BLOCK + jnp.arange(MOBA_BLOCK, dtype=idx.dtype)
    tpos = q_pos[None, None, :, None, None]
    s = jnp.einsum('bqhd,bhqnkd->bhqnk', q, kg).astype(jnp.float32) * ATTN_SCALE
    s = s - slopes[None, :, None, None, None] * (tpos - kpos).astype(jnp.float32)
    s = jnp.where(ok[None, None, :, :, None] & (kpos <= tpos), s, NEG_INF)
    p = jax.nn.softmax(s.reshape(b, h, tq, -1), axis=-1).reshape(s.shape)
    return jnp.einsum('bhqnk,bhqnkd->bqhd', p.astype(vg.dtype), vg)


def moba_attention(q, q_pos, k_all, v_all, slopes):
    b, l, h, d = k_all.shape
    nb = -(-l // MOBA_BLOCK)
    padw = ((0, 0), (0, nb * MOBA_BLOCK - l), (0, 0), (0, 0))
    kb = jnp.pad(k_all, padw).reshape(b, nb, MOBA_BLOCK, h, d).transpose(0, 3, 1, 2, 4)
    vb = jnp.pad(v_all, padw).reshape(b, nb, MOBA_BLOCK, h, d).transpose(0, 3, 1, 2, 4)
    k_mean = jnp.mean(kb.astype(jnp.float32), axis=3)
    n_top = min(MOBA_TOPK, nb)
    fn = lambda qc, pc: moba_block(qc, pc, kb, vb, k_mean, slopes, n_top)
    return sweep_queries(fn, q, q_pos, MOBA_Q_CHUNK)


def memory_attention(q, mem_k, mem_v):
    s = jnp.einsum('bqhd,bmhd->bhqm', q, mem_k).astype(jnp.float32) * ATTN_SCALE
    p = jax.nn.softmax(s, axis=-1)
    return jnp.einsum('bhqm,bmhd->bqhd', p.astype(mem_v.dtype), mem_v)


def memory_kv(mem, w_mem_kv_l):
    b, m, _ = mem.shape
    kv = jnp.einsum('bmd,df->bmf', mem, w_mem_kv_l)
    k, v = jnp.split(kv, 2, axis=-1)
    return k.reshape(b, m, H_MEM, HEAD_DIM), v.reshape(b, m, H_MEM, HEAD_DIM)


def paged_rows(pool, layer, page_table):
    rows = pool[layer, page_table]
    nbat, npg, pg, h, d = rows.shape
    return rows.reshape(nbat, npg * pg, h, d)


def trunk_layer(x, q_pos, past, mem_k, mem_v, w_in_l, g_out_l, w_o_l, ln1_g_l, ln1_b_l,
                w_up_l, w_down_l, ln2_g_l, ln2_b_l):
    b, t, _ = x.shape
    proj = jnp.einsum('btd,df->btf', x, w_in_l)
    cuts = [SB_W, 2 * SB_W, 3 * SB_W, 3 * SB_W + MOBA_W, 3 * SB_W + 2 * MOBA_W, 3 * SB_W + 3 * MOBA_W]
    parts = jnp.split(proj, cuts, axis=-1)
    q_sb, k_sb, v_sb, q_mb, k_mb, v_mb, q_mem = [a.reshape(b, t, -1, HEAD_DIM) for a in parts]
    if past is None:
        k_sb_all, v_sb_all, k_mb_all, v_mb_all = k_sb, v_sb, k_mb, v_mb
    else:
        k_sb_all = jnp.concatenate([past[0], k_sb], axis=1)
        v_sb_all = jnp.concatenate([past[1], v_sb], axis=1)
        k_mb_all = jnp.concatenate([past[2], k_mb], axis=1)
        v_mb_all = jnp.concatenate([past[3], v_mb], axis=1)
    k_pos = jnp.arange(k_sb_all.shape[1], dtype=jnp.int32)
    o_sb = sweep_queries(lambda qc, pc: stick_breaking_block(qc, pc, k_sb_all, v_sb_all, k_pos),
                         q_sb, q_pos, SB_Q_BLOCK)
    o_mb = moba_attention(q_mb, q_pos, k_mb_all, v_mb_all, alibi_slopes(H_MOBA))
    o_mem = memory_attention(q_mem, mem_k, mem_v)
    o = head_rms_norm(jnp.concatenate([o_sb, o_mb, o_mem], axis=2), g_out_l)
    x = layer_norm(DEEPNORM_ALPHA * x + jnp.einsum('btf,fd->btd', o, w_o_l), ln1_g_l, ln1_b_l)
    u = jnp.square(jax.nn.relu(jnp.einsum('btd,df->btf', x, w_up_l)))
    x = layer_norm(DEEPNORM_ALPHA * x + jnp.einsum('btf,fd->btd', u, w_down_l), ln2_g_l, ln2_b_l)
    return x, (k_sb, v_sb, k_mb, v_mb)


def setup_inputs(seed: int = 0) -> dict:
    key = jax.random.key(seed)
    ks = jax.random.split(key, 24)
    n_pages = PAST_LEN // PAGE_SIZE
    n_pool = (DEC_BATCH * n_pages * 5) // 4

    def nrm(k, shape, scale=1.0):
        return jax.random.normal(k, shape, jnp.float32) * scale

    pool_shape_sb = (DEPTH, n_pool, PAGE_SIZE, H_SB, HEAD_DIM)
    pool_shape_mb = (DEPTH, n_pool, PAGE_SIZE, H_MOBA, HEAD_DIM)
    mem_shape = (DEPTH, DEC_BATCH, MEM_LEN, H_MEM, HEAD_DIM)
    page_table = jax.random.permutation(ks[9], n_pool)[:DEC_BATCH * n_pages]
    page_table = page_table.reshape(DEC_BATCH, n_pages).astype(jnp.int32)
    return {
        'x_prompt': nrm(ks[0], (BATCH, SEQ, D_MODEL)),
        'x_sample': nrm(ks[1], (DEC_BATCH, DEC_SEQ, D_MODEL)),
        'mem_prompt': nrm(ks[2], (BATCH, MEM_LEN, D_MODEL)),
        'cache_k_sb': nrm(ks[3], pool_shape_sb),
        'cache_v_sb': nrm(ks[4], pool_shape_sb),
        'cache_k_moba': nrm(ks[5], pool_shape_mb),
        'cache_v_moba': nrm(ks[6], pool_shape_mb),
        'cache_mem_k': nrm(ks[7], mem_shape),
        'cache_mem_v': nrm(ks[8], mem_shape),
        'page_table': page_table,
        'w_in': nrm(ks[10], (DEPTH, D_MODEL, IN_COLS), D_MODEL ** -0.5),
        'w_mem_kv': nrm(ks[11], (DEPTH, D_MODEL, 2 * MEM_W), D_MODEL ** -0.5),
        'out_norm_g': 1.0 + nrm(ks[12], (DEPTH, MIX_W), 0.02),
        'w_o': nrm(ks[13], (DEPTH, MIX_W, D_MODEL), MIX_W ** -0.5 * DEEPNORM_BETA),
        'ln1_g': 1.0 + nrm(ks[14], (DEPTH, D_MODEL), 0.02),
        'ln1_b': nrm(ks[15], (DEPTH, D_MODEL), 0.02),
        'w_up': nrm(ks[16], (DEPTH, D_MODEL, D_FF), D_MODEL ** -0.5),
        'w_down': nrm(ks[17], (DEPTH, D_FF, D_MODEL), D_FF ** -0.5 * DEEPNORM_BETA),
        'ln2_g': 1.0 + nrm(ks[18], (DEPTH, D_MODEL), 0.02),
        'ln2_b': nrm(ks[19], (DEPTH, D_MODEL), 0.02),
    }


def reference(x_prompt, x_sample, mem_prompt, cache_k_sb, cache_v_sb, cache_k_moba, cache_v_moba,
              cache_mem_k, cache_mem_v, page_table, w_in, w_mem_kv, out_norm_g, w_o, ln1_g, ln1_b,
              w_up, w_down, ln2_g, ln2_b):
    pos_p = jnp.arange(x_prompt.shape[1], dtype=jnp.int32)
    past_len = page_table.shape[1] * cache_k_sb.shape[2]
    pos_s = past_len + jnp.arange(x_sample.shape[1], dtype=jnp.int32)
    hp, hs = x_prompt, x_sample
    ksb_p, vsb_p, kmb_p, vmb_p, mk_p, mv_p = [], [], [], [], [], []
    ksb_s, vsb_s, kmb_s, vmb_s = [], [], [], []
    for l in range(DEPTH):
        wl = (w_in[l], out_norm_g[l], w_o[l], ln1_g[l], ln1_b[l], w_up[l], w_down[l], ln2_g[l], ln2_b[l])
        mem_k, mem_v = memory_kv(mem_prompt, w_mem_kv[l])
        hp, rows_p = trunk_layer(hp, pos_p, None, mem_k, mem_v, *wl)
        past = (paged_rows(cache_k_sb, l, page_table), paged_rows(cache_v_sb, l, page_table),
                paged_rows(cache_k_moba, l, page_table), paged_rows(cache_v_moba, l, page_table))
        hs, rows_s = trunk_layer(hs, pos_s, past, cache_mem_k[l], cache_mem_v[l], *wl)
        ksb_p.append(rows_p[0]); vsb_p.append(rows_p[1]); kmb_p.append(rows_p[2]); vmb_p.append(rows_p[3])
        mk_p.append(mem_k); mv_p.append(mem_v)
        ksb_s.append(rows_s[0]); vsb_s.append(rows_s[1]); kmb_s.append(rows_s[2]); vmb_s.append(rows_s[3])
    return (hp, hs,
            jnp.stack(ksb_p), jnp.stack(vsb_p), jnp.stack(kmb_p), jnp.stack(vmb_p),
            jnp.stack(mk_p), jnp.stack(mv_p),
            jnp.stack(ksb_s), jnp.stack(vsb_s), jnp.stack(kmb_s), jnp.stack(vmb_s))
```

```python
import functools

import numpy as np
import jax
import jax.numpy as jnp
from jax import lax
from jax.experimental import pallas as pl
from jax.experimental.pallas import tpu as pltpu

F32 = jnp.float32
BF16 = jnp.bfloat16

HEAD_DIM = 128
MOBA_BLOCK = 256
MOBA_TOPK = 3
LN_EPS = 1e-5
RMS_EPS = 1e-6
NEG_INF = -1e30
ATTN_SCALE = HEAD_DIM ** -0.5
ATTN_TILE = 256
SUFFIX_CHUNK = 128
VMEM_LIMIT = 56 * 1024 * 1024


def _params(*sem):
    return pltpu.CompilerParams(dimension_semantics=sem, vmem_limit_bytes=VMEM_LIMIT)


def _dot_nt(a, b):
    return lax.dot_general(a, b, (((1,), (1,)), ((), ())), preferred_element_type=F32)


def _head_rms(o, g):
    return o * lax.rsqrt(jnp.mean(o * o, axis=-1, keepdims=True) + RMS_EPS) * g


def _layer_norm(y, g, b):
    mu = jnp.mean(y, axis=-1, keepdims=True)
    d = y - mu
    var = jnp.mean(d * d, axis=-1, keepdims=True)
    return d * lax.rsqrt(var + LN_EPS) * g + b


def _neg_softplus(z):
    return -(jnp.maximum(z, 0.0) + jnp.log(1.0 + jnp.exp(-jnp.abs(z))))


def _split(x):
    hi = x.astype(BF16)
    return hi, (x - hi.astype(F32)).astype(BF16)


def _suffix_sums(lk, uu):
    hi, lo = _split(lk)
    loc = jnp.dot(jnp.concatenate([hi, lo], axis=1), uu, preferred_element_type=F32)
    return loc, loc[:, 0:1] + lk[:, 0:1]


def _suffix_matrix():
    j = np.arange(SUFFIX_CHUNK)
    u = (j[:, None] > j[None, :]).astype(np.float32)
    return jnp.asarray(np.concatenate([u, u], axis=0), dtype=BF16)


def _proj_kernel(x_ref, w_ref, o_ref, xb_ref):
    @pl.when(pl.program_id(1) == 0)
    def _():
        xb_ref[...] = x_ref[...].astype(BF16)

    o_ref[...] = jnp.dot(xb_ref[...], w_ref[...], preferred_element_type=F32)


def _proj(x, w_stack, layer, tm, tn):
    m, k = x.shape
    n = w_stack.shape[2]
    tm, tn = min(tm, m), min(tn, n)
    return pl.pallas_call(
        _proj_kernel,
        grid=(m // tm, n // tn),
        in_specs=[pl.BlockSpec((tm, k), lambda i, j: (i, 0)),
                  pl.BlockSpec((None, k, tn), lambda i, j: (layer, 0, j))],
        out_specs=pl.BlockSpec((tm, tn), lambda i, j: (i, j)),
        out_shape=jax.ShapeDtypeStruct((m, n), F32),
        scratch_shapes=[pltpu.VMEM((tm, k), BF16)],
        compiler_params=_params("parallel", "arbitrary"),
        name="proj",
    )(x, w_stack)


def _out_ln_kernel(x_ref, w_ref, r_ref, g_ref, b_ref, o_ref, acc_ref, *, alpha):
    k = pl.program_id(1)

    @pl.when(k == 0)
    def _():
        acc_ref[...] = jnp.zeros_like(acc_ref)

    acc_ref[...] += jnp.dot(x_ref[...], w_ref[...], preferred_element_type=F32)

    @pl.when(k == pl.num_programs(1) - 1)
    def _():
        o_ref[...] = _layer_norm(alpha * r_ref[...] + acc_ref[...], g_ref[...], b_ref[...])


def _out_ln(x, w_stack, res, g, b, layer, alpha, tm, tk):
    m, kdim = x.shape
    n = w_stack.shape[2]
    tm, tk = min(tm, m), min(tk, kdim)
    return pl.pallas_call(
        functools.partial(_out_ln_kernel, alpha=alpha),
        grid=(m // tm, kdim // tk),
        in_specs=[pl.BlockSpec((tm, tk), lambda i, k: (i, k)),
                  pl.BlockSpec((None, tk, n), lambda i, k: (layer, k, 0)),
                  pl.BlockSpec((tm, n), lambda i, k: (i, 0)),
                  pl.BlockSpec((None, 1, n), lambda i, k: (layer, 0, 0)),
                  pl.BlockSpec((None, 1, n), lambda i, k: (layer, 0, 0))],
        out_specs=pl.BlockSpec((tm, n), lambda i, k: (i, 0)),
        out_shape=jax.ShapeDtypeStruct((m, n), F32),
        scratch_shapes=[pltpu.VMEM((tm, n), F32)],
        compiler_params=_params("parallel", "arbitrary"),
        name="out_ln",
    )(x, w_stack, res, g, b)


def _mlp_kernel(x_ref, wu_ref, wd_ref, g_ref, b_ref, o_ref, acc_ref, xb_ref, *, alpha):
    f = pl.program_id(1)

    @pl.when(f == 0)
    def _():
        xb_ref[...] = x_ref[...].astype(BF16)
        acc_ref[...] = jnp.zeros_like(acc_ref)

    h = jnp.maximum(jnp.dot(xb_ref[...], wu_ref[...], preferred_element_type=F32), 0.0)
    acc_ref[...] += jnp.dot((h * h).astype(BF16), wd_ref[...], preferred_element_type=F32)

    @pl.when(f == pl.num_programs(1) - 1)
    def _():
        o_ref[...] = _layer_norm(alpha * x_ref[...] + acc_ref[...], g_ref[...], b_ref[...])


def _mlp(x, wu_stack, wd_stack, g, b, layer, alpha, tm, tf):
    m, d = x.shape
    ff = wu_stack.shape[2]
    tm, tf = min(tm, m), min(tf, ff)
    return pl.pallas_call(
        functools.partial(_mlp_kernel, alpha=alpha),
        grid=(m // tm, ff // tf),
        in_specs=[pl.BlockSpec((tm, d), lambda i, f: (i, 0)),
                  pl.BlockSpec((None, d, tf), lambda i, f: (layer, 0, f)),
                  pl.BlockSpec((None, tf, d), lambda i, f: (layer, f, 0)),
                  pl.BlockSpec((None, 1, d), lambda i, f: (layer, 0, 0)),
                  pl.BlockSpec((None, 1, d), lambda i, f: (layer, 0, 0))],
        out_specs=pl.BlockSpec((tm, d), lambda i, f: (i, 0)),
        out_shape=jax.ShapeDtypeStruct((m, d), F32),
        scratch_shapes=[pltpu.VMEM((tm, d), F32), pltpu.VMEM((tm, d), BF16)],
        compiler_params=_params("parallel", "arbitrary"),
        name="mlp",
    )(x, wu_stack, wd_stack, g, b)


def _causal_pairs(n_blocks):
    qi, kj = [], []
    for i in range(n_blocks):
        for j in range(i, -1, -1):
            qi.append(i)
            kj.append(j)
    return jnp.asarray(qi, jnp.int32), jnp.asarray(kj, jnp.int32)


def _sb_prompt_kernel(qi_ref, kj_ref, q_ref, k_ref, v_ref, uu_ref, g_ref, o_ref,
                      acc_ref, carry_ref, qb_ref):
    p = pl.program_id(2)
    qi, kj = qi_ref[p], kj_ref[p]
    t = ATTN_TILE

    @pl.when(kj == qi)
    def _():
        acc_ref[...] = jnp.zeros_like(acc_ref)
        carry_ref[...] = jnp.zeros_like(carry_ref)
        qb_ref[...] = q_ref[...].astype(BF16)

    def tile(diagonal):
        qb = qb_ref[...]
        uu = uu_ref[...]
        carry = carry_ref[...]
        acc = acc_ref[...]
        row = lax.broadcasted_iota(jnp.int32, (t, SUFFIX_CHUNK), 0)
        col = lax.broadcasted_iota(jnp.int32, (t, SUFFIX_CHUNK), 1)
        for c in reversed(range(t // SUFFIX_CHUNK)):
            rows = slice(c * SUFFIX_CHUNK, (c + 1) * SUFFIX_CHUNK)
            z = _dot_nt(qb, k_ref[rows, :].astype(BF16)) * ATTN_SCALE
            nsp = _neg_softplus(z)
            if diagonal:
                past = (col + c * SUFFIX_CHUNK) < row
                lk = jnp.where(past, nsp, 0.0)
            else:
                lk = nsp
            loc, tot = _suffix_sums(lk, uu)
            w = jnp.exp(z + nsp + loc + carry)
            if diagonal:
                w = jnp.where(past, w, 0.0)
            acc = acc + jnp.dot(w.astype(BF16), v_ref[rows, :].astype(BF16), preferred_element_type=F32)
            carry = carry + tot
        acc_ref[...] = acc
        carry_ref[...] = carry

    @pl.when(kj == qi)
    def _():
        tile(True)

    @pl.when(kj != qi)
    def _():
        tile(False)

    @pl.when(kj == 0)
    def _():
        o_ref[...] = _head_rms(acc_ref[...], g_ref[...]).astype(o_ref.dtype)


def _moba_prompt_kernel(qi_ref, kj_ref, slope_ref, q_ref, k_ref, v_ref, km_ref, g_ref, o_ref,
                        acc_ref, m_ref, l_ref, qb_ref, gate_ref, *, n_top):
    h = pl.program_id(1)
    p = pl.program_id(2)
    qi, kj = qi_ref[p], kj_ref[p]
    t = ATTN_TILE
    slope = slope_ref[h]

    @pl.when(kj == qi)
    def _():
        acc_ref[...] = jnp.zeros_like(acc_ref)
        m_ref[...] = jnp.full_like(m_ref, NEG_INF)
        l_ref[...] = jnp.zeros_like(l_ref)
        qb_ref[...] = q_ref[...].astype(BF16)
        gate_ref[...] = lax.dot_general(q_ref[...], km_ref[...], (((1,), (1,)), ((), ())),
                                        preferred_element_type=F32, precision=lax.Precision.HIGHEST)

    def tile(mask, dist0):
        s = _dot_nt(qb_ref[...], k_ref[...].astype(BF16)) * ATTN_SCALE
        row = lax.broadcasted_iota(jnp.int32, (t, t), 0)
        col = lax.broadcasted_iota(jnp.int32, (t, t), 1)
        dist = (row - col).astype(F32) + dist0
        s = s - slope * dist
        if mask is None:
            mask = col <= row
        s = jnp.where(mask, s, NEG_INF)
        m_old = m_ref[...]
        m_new = jnp.maximum(m_old, jnp.max(s, axis=-1, keepdims=True))
        a = jnp.exp(m_old - m_new)
        e = jnp.exp(s - m_new)
        l_ref[...] = a * l_ref[...] + jnp.sum(e, axis=-1, keepdims=True)
        acc_ref[...] = a * acc_ref[...] + jnp.dot(e.astype(BF16), v_ref[...].astype(BF16),
                                                  preferred_element_type=F32)
        m_ref[...] = m_new

    @pl.when(kj == qi)
    def _():
        tile(None, 0.0)

    @pl.when(kj != qi)
    def _():
        gate = gate_ref[...]
        lane = lax.broadcasted_iota(jnp.int32, gate.shape, 1)
        g_j = jnp.sum(jnp.where(lane == kj, gate, 0.0), axis=-1, keepdims=True)
        beats = ((gate > g_j) | ((gate == g_j) & (lane < kj))) & (lane < qi)
        rank = jnp.sum(jnp.where(beats, 1.0, 0.0), axis=-1, keepdims=True)
        tile(rank < n_top, ((qi - kj) * t).astype(F32))

    @pl.when(kj == 0)
    def _():
        o_ref[...] = _head_rms(acc_ref[...] / l_ref[...], g_ref[...]).astype(o_ref.dtype)


def _mem_prompt_kernel(q_ref, k_ref, v_ref, g_ref, o_ref):
    s = _dot_nt(q_ref[...].astype(BF16), k_ref[...].astype(BF16)) * ATTN_SCALE
    e = jnp.exp(s - jnp.max(s, axis=-1, keepdims=True))
    o = jnp.dot(e.astype(BF16), v_ref[...].astype(BF16), preferred_element_type=F32)
    o = o / jnp.sum(e, axis=-1, keepdims=True)
    o_ref[...] = _head_rms(o, g_ref[...]).astype(o_ref.dtype)


def _block_mean_kernel(k_ref, o_ref, *, group):
    x = k_ref[...]
    o_ref[...] = jnp.sum(x.reshape(group, MOBA_BLOCK, x.shape[-1]), axis=1) * (1.0 / MOBA_BLOCK)


def _prompt_attention(proj, memkv, gain, slopes, uu, layer, batch, seq, h_sb, h_mb, h_mem, mem_len):
    t = ATTN_TILE
    nblk = seq // t
    rows = batch * seq
    qi_tab, kj_tab = _causal_pairs(nblk)
    npairs = int(qi_tab.shape[0])
    hd = HEAD_DIM
    c_qsb, c_ksb, c_vsb = 0, h_sb, 2 * h_sb
    c_qmb, c_kmb, c_vmb = 3 * h_sb, 3 * h_sb + h_mb, 3 * h_sb + 2 * h_mb
    c_qmem = 3 * h_sb + 3 * h_mb
    assert c_kmb % h_mb == 0

    def qmap(off):
        return lambda b, h, p, qi, kj: (b * nblk + qi[p], off + h)

    def kmap(off):
        return lambda b, h, p, qi, kj: (b * nblk + kj[p], off + h)

    def gmap(off):
        return lambda b, h, p, qi, kj: (layer, 0, off + h)

    o_sb = pl.pallas_call(
        _sb_prompt_kernel,
        grid_spec=pltpu.PrefetchScalarGridSpec(
            num_scalar_prefetch=2, grid=(batch, h_sb, npairs),
            in_specs=[pl.BlockSpec((t, hd), qmap(c_qsb)),
                      pl.BlockSpec((t, hd), kmap(c_ksb)),
                      pl.BlockSpec((t, hd), kmap(c_vsb)),
                      pl.BlockSpec((2 * SUFFIX_CHUNK, SUFFIX_CHUNK), lambda b, h, p, qi, kj: (0, 0)),
                      pl.BlockSpec((None, 1, hd), gmap(0))],
            out_specs=pl.BlockSpec((t, hd), lambda b, h, p, qi, kj: (b * nblk + qi[p], h)),
            scratch_shapes=[pltpu.VMEM((t, hd), F32), pltpu.VMEM((t, 1), F32), pltpu.VMEM((t, hd), BF16)]),
        out_shape=jax.ShapeDtypeStruct((rows, h_sb * hd), BF16),
        compiler_params=_params("parallel", "parallel", "arbitrary"),
        name="prompt_sb_attn",
    )(qi_tab, kj_tab, proj, proj, proj, uu, gain)

    group = min(8, batch * nblk)
    kmean = pl.pallas_call(
        functools.partial(_block_mean_kernel, group=group),
        grid=(batch * nblk // group,),
        in_specs=[pl.BlockSpec((group * MOBA_BLOCK, h_mb * hd), lambda i: (i, c_kmb // h_mb))],
        out_specs=pl.BlockSpec((group, h_mb * hd), lambda i: (i, 0)),
        out_shape=jax.ShapeDtypeStruct((batch * nblk, h_mb * hd), F32),
        compiler_params=_params("parallel"),
        name="moba_block_mean",
    )(proj)
    kmean = kmean.reshape(batch, nblk, h_mb, hd).transpose(0, 2, 1, 3)
    kmean = jnp.pad(kmean, ((0, 0), (0, 0), (0, hd - nblk), (0, 0)))

    o_mb = pl.pallas_call(
        functools.partial(_moba_prompt_kernel, n_top=min(MOBA_TOPK, nblk)),
        grid_spec=pltpu.PrefetchScalarGridSpec(
            num_scalar_prefetch=2, grid=(batch, h_mb, npairs),
            in_specs=[pl.BlockSpec(memory_space=pltpu.SMEM),
                      pl.BlockSpec((t, hd), qmap(c_qmb)),
                      pl.BlockSpec((t, hd), kmap(c_kmb)),
                      pl.BlockSpec((t, hd), kmap(c_vmb)),
                      pl.BlockSpec((None, None, hd, hd), lambda b, h, p, qi, kj: (b, h, 0, 0)),
                      pl.BlockSpec((None, 1, hd), gmap(h_sb))],
            out_specs=pl.BlockSpec((t, hd), lambda b, h, p, qi, kj: (b * nblk + qi[p], h)),
            scratch_shapes=[pltpu.VMEM((t, hd), F32), pltpu.VMEM((t, 1), F32), pltpu.VMEM((t, 1), F32),
                            pltpu.VMEM((t, hd), BF16), pltpu.VMEM((t, hd), F32)]),
        out_shape=jax.ShapeDtypeStruct((rows, h_mb * hd), BF16),
        compiler_params=_params("parallel", "parallel", "arbitrary"),
        name="prompt_moba_attn",
    )(qi_tab, kj_tab, slopes, proj, proj, proj, kmean, gain)

    tq = min(512, seq)
    nq = seq // tq
    o_mem = pl.pallas_call(
        _mem_prompt_kernel,
        grid=(batch, h_mem, nq),
        in_specs=[pl.BlockSpec((tq, hd), lambda b, h, i: (b * nq + i, c_qmem + h)),
                  pl.BlockSpec((mem_len, hd), lambda b, h, i: (b, h)),
                  pl.BlockSpec((mem_len, hd), lambda b, h, i: (b, h_mem + h)),
                  pl.BlockSpec((None, 1, hd), lambda b, h, i: (layer, 0, h_sb + h_mb + h))],
        out_specs=pl.BlockSpec((tq, hd), lambda b, h, i: (b * nq + i, h)),
        out_shape=jax.ShapeDtypeStruct((rows, h_mem * hd), BF16),
        compiler_params=_params("parallel", "parallel", "parallel"),
        name="prompt_mem_attn",
    )(proj, memkv, memkv, gain)
    return jnp.concatenate([o_sb, o_mb, o_mem], axis=1)


def _dot3(x, w):
    xh, xl = _split(x)
    wh, wl = _split(w)
    m = x.shape[0]
    a = jnp.dot(jnp.concatenate([xh, xl], axis=0), wh, preferred_element_type=F32)
    return a[:m] + a[m:] + jnp.dot(xh, wl, preferred_element_type=F32)


def _dot3_nt(x, k):
    xh, xl = _split(x)
    kh, kl = _split(k)
    m = x.shape[0]
    a = _dot_nt(jnp.concatenate([xh, xl], axis=0), kh)
    return a[:m] + a[m:] + _dot_nt(xh, kl)


def _proj3_kernel(x_ref, w_ref, o_ref):
    o_ref[...] = _dot3(x_ref[...], w_ref[...])


def _proj3(x, w_stack, layer, tn):
    m, k = x.shape
    n = w_stack.shape[2]
    tn = min(tn, n)
    return pl.pallas_call(
        _proj3_kernel,
        grid=(n // tn,),
        in_specs=[pl.BlockSpec((m, k), lambda j: (0, 0)),
                  pl.BlockSpec((None, k, tn), lambda j: (layer, 0, j))],
        out_specs=pl.BlockSpec((m, tn), lambda j: (0, j)),
        out_shape=jax.ShapeDtypeStruct((m, n), F32),
        compiler_params=_params("parallel"),
        name="sample_proj",
    )(x, w_stack)


def _out_ln3_kernel(x_ref, w_ref, r_ref, g_ref, b_ref, o_ref, acc_ref, *, alpha):
    k = pl.program_id(0)

    @pl.when(k == 0)
    def _():
        acc_ref[...] = jnp.zeros_like(acc_ref)

    acc_ref[...] += _dot3(x_ref[...], w_ref[...])

    @pl.when(k == pl.num_programs(0) - 1)
    def _():
        o_ref[...] = _layer_norm(alpha * r_ref[...] + acc_ref[...], g_ref[...], b_ref[...])


def _out_ln3(x, w_stack, res, g, b, layer, alpha, tk):
    m, kdim = x.shape
    n = w_stack.shape[2]
    tk = min(tk, kdim)
    return pl.pallas_call(
        functools.partial(_out_ln3_kernel, alpha=alpha),
        grid=(kdim // tk,),
        in_specs=[pl.BlockSpec((m, tk), lambda k: (0, k)),
                  pl.BlockSpec((None, tk, n), lambda k: (layer, k, 0)),
                  pl.BlockSpec((m, n), lambda k: (0, 0)),
                  pl.BlockSpec((None, 1, n), lambda k: (layer, 0, 0)),
                  pl.BlockSpec((None, 1, n), lambda k: (layer, 0, 0))],
        out_specs=pl.BlockSpec((m, n), lambda k: (0, 0)),
        out_shape=jax.ShapeDtypeStruct((m, n), F32),
        scratch_shapes=[pltpu.VMEM((m, n), F32)],
        compiler_params=_params("arbitrary"),
        name="sample_out_ln",
    )(x, w_stack, res, g, b)


def _mlp3_kernel(x_ref, wu_ref, wd_ref, g_ref, b_ref, o_ref, acc_ref, *, alpha):
    f = pl.program_id(0)

    @pl.when(f == 0)
    def _():
        acc_ref[...] = jnp.zeros_like(acc_ref)

    h = jnp.maximum(_dot3(x_ref[...], wu_ref[...]), 0.0)
    acc_ref[...] += _dot3(h * h, wd_ref[...])

    @pl.when(f == pl.num_programs(0) - 1)
    def _():
        o_ref[...] = _layer_norm(alpha * x_ref[...] + acc_ref[...], g_ref[...], b_ref[...])


def _mlp3(x, wu_stack, wd_stack, g, b, layer, alpha, tf):
    m, d = x.shape
    ff = wu_stack.shape[2]
    tf = min(tf, ff)
    return pl.pallas_call(
        functools.partial(_mlp3_kernel, alpha=alpha),
        grid=(ff // tf,),
        in_specs=[pl.BlockSpec((m, d), lambda f: (0, 0)),
                  pl.BlockSpec((None, d, tf), lambda f: (layer, 0, f)),
                  pl.BlockSpec((None, tf, d), lambda f: (layer, f, 0)),
                  pl.BlockSpec((None, 1, d), lambda f: (layer, 0, 0)),
                  pl.BlockSpec((None, 1, d), lambda f: (layer, 0, 0))],
        out_specs=pl.BlockSpec((m, d), lambda f: (0, 0)),
        out_shape=jax.ShapeDtypeStruct((m, d), F32),
        scratch_shapes=[pltpu.VMEM((m, d), F32)],
        compiler_params=_params("arbitrary"),
        name="sample_mlp",
    )(x, wu_stack, wd_stack, g, b)


def _pick_rows(parts):
    row = lax.broadcasted_iota(jnp.int32, parts[0].shape, 0)
    out = jnp.zeros_like(parts[0])
    for h, part in enumerate(parts):
        out = jnp.where(row == h, part, out)
    return out


def _head_scores(q, keys):
    qh, ql = _split(q)
    q2 = jnp.concatenate([qh, ql], axis=0)
    parts = []
    for k in keys:
        kh, kl = _split(k)
        a = _dot_nt(q2, kh)
        parts.append(a[:8] + a[8:] + _dot_nt(qh, kl))
    return _pick_rows(parts) * ATTN_SCALE


def _head_values(w, values):
    wh, wl = _split(w)
    w2 = jnp.concatenate([wh, wl], axis=0)
    parts = []
    for v in values:
        vh, vl = _split(v)
        a = jnp.dot(w2, vh, preferred_element_type=F32)
        parts.append(a[:8] + a[8:] + jnp.dot(wh, vl, preferred_element_type=F32))
    return _pick_rows(parts)


def _suffix_sums3(lk, uu):
    hi, lo = _split(lk)
    lo2 = (lk - hi.astype(F32) - lo.astype(F32)).astype(BF16)
    loc = jnp.dot(jnp.concatenate([hi, lo], axis=1), uu, preferred_element_type=F32)
    loc = loc + jnp.dot(lo2, uu[:SUFFIX_CHUNK], preferred_element_type=F32)
    return loc, loc[:, 0:1] + lk[:, 0:1]


def _sb_decode_kernel(pt_ref, q_ref, k_ref, v_ref, uu_ref, g_ref, o_ref, acc_ref, carry_ref, *, n_heads):
    j = pl.program_id(1)

    @pl.when(j == 0)
    def _():
        acc_ref[...] = jnp.zeros_like(acc_ref)
        carry_ref[...] = jnp.zeros_like(carry_ref)

    z = _head_scores(q_ref[...], [k_ref[h] for h in range(n_heads)])
    nsp = _neg_softplus(z)
    loc, tot = _suffix_sums3(nsp, uu_ref[...])
    w = jnp.exp(z + nsp + loc + carry_ref[...])
    acc_ref[...] += _head_values(w, [v_ref[h] for h in range(n_heads)])
    carry_ref[...] += tot

    @pl.when(j == pl.num_programs(1) - 1)
    def _():
        o_ref[...] = _head_rms(acc_ref[...], g_ref[...])


def _moba_gate_kernel(pt_ref, q_ref, *refs, n_top, pages_per_block):
    k_refs = refs[:pages_per_block]
    idx_ref, gate_ref = refs[pages_per_block:]
    n = pl.program_id(1)

    @pl.when(n == 0)
    def _():
        gate_ref[...] = jnp.full_like(gate_ref, NEG_INF)

    mean = sum(jnp.sum(k_ref[...], axis=1) for k_ref in k_refs) * (1.0 / MOBA_BLOCK)
    n_heads = mean.shape[0]
    g = jnp.sum(q_ref[0:n_heads, :] * mean, axis=-1, keepdims=True)
    lane = lax.broadcasted_iota(jnp.int32, (n_heads, HEAD_DIM), 1)
    gate_ref[0:n_heads, :] = jnp.where(lane == n, g, gate_ref[0:n_heads, :])

    @pl.when(n == pl.num_programs(1) - 1)
    def _():
        gate = gate_ref[...]
        lanes = lax.broadcasted_iota(jnp.int32, gate.shape, 1)
        idx = jnp.zeros(gate.shape, jnp.int32)
        for r in range(n_top):
            best = jnp.max(gate, axis=-1, keepdims=True)
            first = jnp.min(jnp.where(gate == best, lanes, HEAD_DIM), axis=-1, keepdims=True)
            idx = jnp.where(lanes == r, first, idx)
            gate = jnp.where(lanes == first, NEG_INF, gate)
        idx_ref[...] = idx


def _moba_decode_kernel(pt_ref, top_ref, slope_ref, q_ref, kn_ref, vn_ref, g_ref, k_ref, v_ref, o_ref,
                        acc_ref, m_ref, l_ref, *, past_len, page):
    s_id, h, t = pl.program_id(0), pl.program_id(1), pl.program_id(2)
    pages_per_block = MOBA_BLOCK // page
    q = q_ref[...]

    @pl.when(t == 0)
    def _():
        m_ref[...] = jnp.sum(q * kn_ref[...], axis=-1, keepdims=True) * ATTN_SCALE
        l_ref[...] = jnp.ones_like(l_ref)
        acc_ref[...] = vn_ref[...]

    q8 = jnp.broadcast_to(q, (8, HEAD_DIM))
    s = _dot3_nt(q8, k_ref[...])[0:1] * ATTN_SCALE
    lane = lax.broadcasted_iota(jnp.int32, (1, page), 1)
    kpos0 = top_ref[s_id, h, t // pages_per_block] * MOBA_BLOCK + (t % pages_per_block) * page
    s = s - slope_ref[h] * (past_len - kpos0 - lane).astype(F32)
    m_old = m_ref[...]
    m_new = jnp.maximum(m_old, jnp.max(s, axis=-1, keepdims=True))
    a = jnp.exp(m_old - m_new)
    e = jnp.exp(s - m_new)
    l_ref[...] = a * l_ref[...] + jnp.sum(e, axis=-1, keepdims=True)
    acc_ref[...] = a * acc_ref[...] + _dot3(jnp.broadcast_to(e, (8, page)), v_ref[...])[0:1]
    m_ref[...] = m_new

    @pl.when(t == pl.num_programs(2) - 1)
    def _():
        o_ref[...] = _head_rms(acc_ref[...] / l_ref[...], g_ref[...])


def _mem_decode_kernel(q_ref, k_ref, v_ref, g_ref, o_ref, *, n_heads):
    s = _head_scores(q_ref[...], [k_ref[:, h, :] for h in range(n_heads)])
    e = jnp.exp(s - jnp.max(s, axis=-1, keepdims=True))
    o = _head_values(e, [v_ref[:, h, :] for h in range(n_heads)]) / jnp.sum(e, axis=-1, keepdims=True)
    o_ref[...] = _head_rms(o, g_ref[...])


def _heads_to_rows(x, n_heads):
    x = x.reshape(x.shape[0], n_heads, HEAD_DIM)
    return jnp.pad(x, ((0, 0), (0, 8 - n_heads), (0, 0)))


def _sample_attention(proj, gain_l, page_table, pools, mem_k, mem_v, slopes, uu, layer, h_sb, h_mb, h_mem):
    cache_k_sb, cache_v_sb, cache_k_mb, cache_v_mb = pools
    n_seq, n_pages = page_table.shape
    page = cache_k_sb.shape[3]
    past_len = n_pages * page
    hd = HEAD_DIM
    w_sb, w_mb = h_sb * hd, h_mb * hd
    q_sb = _heads_to_rows(proj[:, 0:w_sb], h_sb)
    q_mb = proj[:, 3 * w_sb:3 * w_sb + w_mb]
    k_mb_new = proj[:, 3 * w_sb + w_mb:3 * w_sb + 2 * w_mb].reshape(n_seq, h_mb, 1, hd)
    v_mb_new = proj[:, 3 * w_sb + 2 * w_mb:3 * w_sb + 3 * w_mb].reshape(n_seq, h_mb, 1, hd)
    q_mem = _heads_to_rows(proj[:, 3 * w_sb + 3 * w_mb:], h_mem)
    g_sb = _heads_to_rows(gain_l[None, 0:w_sb], h_sb)[0]
    g_mb = gain_l[w_sb:w_sb + w_mb].reshape(h_mb, 1, hd)
    g_mem = _heads_to_rows(gain_l[None, w_sb + w_mb:], h_mem)[0]

    row_spec = pl.BlockSpec((None, 8, hd), lambda s, j, *_: (s, 0, 0))
    gain_spec = pl.BlockSpec((8, hd), lambda s, j, *_: (0, 0))
    row_shape = jax.ShapeDtypeStruct((n_seq, 8, hd), F32)

    def page_spec(n_heads, page_of):
        return pl.BlockSpec((None, None, n_heads, page, hd),
                            lambda s, j, pt: (layer, page_of(s, j, pt), 0, 0, 0))

    newest_first = lambda s, j, pt: pt[s, n_pages - 1 - j]
    o_sb = pl.pallas_call(
        functools.partial(_sb_decode_kernel, n_heads=h_sb),
        grid_spec=pltpu.PrefetchScalarGridSpec(
            num_scalar_prefetch=1, grid=(n_seq, n_pages),
            in_specs=[row_spec, page_spec(h_sb, newest_first), page_spec(h_sb, newest_first),
                      pl.BlockSpec((2 * SUFFIX_CHUNK, SUFFIX_CHUNK), lambda s, j, pt: (0, 0)), gain_spec],
            out_specs=row_spec,
            scratch_shapes=[pltpu.VMEM((8, hd), F32), pltpu.VMEM((8, 1), F32)]),
        out_shape=row_shape,
        compiler_params=_params("parallel", "arbitrary"),
        name="sample_sb_attn",
    )(page_table, q_sb, cache_k_sb, cache_v_sb, uu, g_sb)

    pages_per_block = MOBA_BLOCK // page
    n_past_blocks = past_len // MOBA_BLOCK
    n_top = min(MOBA_TOPK, n_past_blocks)
    top, _ = pl.pallas_call(
        functools.partial(_moba_gate_kernel, n_top=n_top, pages_per_block=pages_per_block),
        grid_spec=pltpu.PrefetchScalarGridSpec(
            num_scalar_prefetch=1, grid=(n_seq, n_past_blocks),
            in_specs=[row_spec] + [page_spec(h_mb, functools.partial(
                lambda s, n, pt, i: pt[s, pages_per_block * n + i], i=i)) for i in range(pages_per_block)],
            out_specs=[row_spec, row_spec]),
        out_shape=[jax.ShapeDtypeStruct((n_seq, 8, hd), jnp.int32), row_shape],
        compiler_params=_params("parallel", "arbitrary"),
        name="sample_moba_gate",
    )(page_table, _heads_to_rows(q_mb, h_mb), *([cache_k_mb] * pages_per_block))
    top = top[:, :, :max(n_top, 1)]

    def sel_page(s, h, t, pt, tp):
        return (layer, pt[s, tp[s, h, t // pages_per_block] * pages_per_block + t % pages_per_block], h, 0, 0)

    head_row = pl.BlockSpec((None, None, 1, hd), lambda s, h, t, pt, tp: (s, h, 0, 0))
    o_mb = pl.pallas_call(
        functools.partial(_moba_decode_kernel, past_len=past_len, page=page),
        grid_spec=pltpu.PrefetchScalarGridSpec(
            num_scalar_prefetch=2, grid=(n_seq, h_mb, n_top * pages_per_block),
            in_specs=[pl.BlockSpec(memory_space=pltpu.SMEM), head_row, head_row, head_row,
                      pl.BlockSpec((None, 1, hd), lambda s, h, t, pt, tp: (h, 0, 0)),
                      pl.BlockSpec((None, None, None, page, hd), sel_page),
                      pl.BlockSpec((None, None, None, page, hd), sel_page)],
            out_specs=head_row,
            scratch_shapes=[pltpu.VMEM((1, hd), F32), pltpu.VMEM((1, 1), F32), pltpu.VMEM((1, 1), F32)]),
        out_shape=jax.ShapeDtypeStruct((n_seq, h_mb, 1, hd), F32),
        compiler_params=_params("parallel", "parallel", "arbitrary"),
        name="sample_moba_attn",
    )(page_table, top, slopes, q_mb.reshape(n_seq, h_mb, 1, hd), k_mb_new, v_mb_new, g_mb, cache_k_mb, cache_v_mb)

    mem_len = mem_k.shape[2]
    mem_spec = pl.BlockSpec((None, None, mem_len, h_mem, hd), lambda s: (layer, s, 0, 0, 0))
    o_mem = pl.pallas_call(
        functools.partial(_mem_decode_kernel, n_heads=h_mem),
        grid=(n_seq,),
        in_specs=[pl.BlockSpec((None, 8, hd), lambda s: (s, 0, 0)), mem_spec, mem_spec,
                  pl.BlockSpec((8, hd), lambda s: (0, 0))],
        out_specs=pl.BlockSpec((None, 8, hd), lambda s: (s, 0, 0)),
        out_shape=row_shape,
        compiler_params=_params("parallel"),
        name="sample_mem_attn",
    )(q_mem, mem_k, mem_v, g_mem)

    return jnp.concatenate([o_sb[:, :h_sb].reshape(n_seq, w_sb), o_mb.reshape(n_seq, w_mb),
                            o_mem[:, :h_mem].reshape(n_seq, h_mem * hd)], axis=1)


def kernel(x_prompt, x_sample, mem_prompt, cache_k_sb, cache_v_sb, cache_k_moba, cache_v_moba, cache_mem_k, cache_mem_v, page_table, w_in, w_mem_kv, out_norm_g, w_o, ln1_g, ln1_b, w_up, w_down, ln2_g, ln2_b):
    batch, seq, d_model = x_prompt.shape
    n_seq, dec_seq, _ = x_sample.shape
    depth = w_in.shape[0]
    h_sb, h_mb, h_mem = cache_k_sb.shape[3], cache_k_moba.shape[3], cache_mem_k.shape[3]
    mem_len = mem_prompt.shape[1]
    hd = HEAD_DIM
    assert dec_seq == 1 and cache_k_sb.shape[4] == hd and seq % ATTN_TILE == 0
    assert MOBA_BLOCK % cache_k_sb.shape[2] == 0 and seq // ATTN_TILE <= hd
    w_sb, w_mb, w_mem = h_sb * hd, h_mb * hd, h_mem * hd
    alpha = (2 * depth) ** 0.25

    w_in_b, w_kv_b, w_o_b = w_in.astype(BF16), w_mem_kv.astype(BF16), w_o.astype(BF16)
    w_up_b, w_down_b = w_up.astype(BF16), w_down.astype(BF16)
    gain = out_norm_g.reshape(depth, 1, d_model)
    ln1_g3, ln1_b3 = ln1_g.reshape(depth, 1, d_model), ln1_b.reshape(depth, 1, d_model)
    ln2_g3, ln2_b3 = ln2_g.reshape(depth, 1, d_model), ln2_b.reshape(depth, 1, d_model)
    slopes = jnp.exp2(-8.0 * jnp.arange(1, h_mb + 1, dtype=F32) / h_mb)
    uu = _suffix_matrix()
    pools = tuple(jnp.transpose(c, (0, 1, 3, 2, 4)) for c in (cache_k_sb, cache_v_sb, cache_k_moba, cache_v_moba))

    hp = x_prompt.reshape(batch * seq, d_model)
    hs = x_sample.reshape(n_seq, d_model)
    mem2d = mem_prompt.reshape(batch * mem_len, d_model)
    outs = [[] for _ in range(10)]
    for l in range(depth):
        memkv = _proj(mem2d, w_kv_b, l, 512, 512)
        proj_p = _proj(hp, w_in_b, l, 1024, 512)
        mix_p = _prompt_attention(proj_p, memkv, gain, slopes, uu, l, batch, seq, h_sb, h_mb, h_mem, mem_len)
        hp = _out_ln(mix_p, w_o_b, hp, ln1_g3, ln1_b3, l, alpha, 512, 512)
        hp = _mlp(hp, w_up_b, w_down_b, ln2_g3, ln2_b3, l, alpha, 512, 512)

        proj_s = _proj3(hs, w_in, l, 512)
        mix_s = _sample_attention(proj_s, out_norm_g[l], page_table, pools, cache_mem_k, cache_mem_v,
                                  slopes, uu, l, h_sb, h_mb, h_mem)
        hs = _out_ln3(mix_s, w_o, hs, ln1_g3, ln1_b3, l, alpha, 512)
        hs = _mlp3(hs, w_up, w_down, ln2_g3, ln2_b3, l, alpha, 512)

        for i, (lo, n_h) in enumerate(((w_sb, h_sb), (2 * w_sb, h_sb),
                                       (3 * w_sb + w_mb, h_mb), (3 * w_sb + 2 * w_mb, h_mb))):
            outs[i].append(proj_p[:, lo:lo + n_h * hd].reshape(batch, seq, n_h, hd))
            outs[6 + i].append(proj_s[:, lo:lo + n_h * hd].reshape(n_seq, 1, n_h, hd))
        outs[4].append(memkv[:, :w_mem].reshape(batch, mem_len, h_mem, hd))
        outs[5].append(memkv[:, w_mem:].reshape(batch, mem_len, h_mem, hd))
    return (hp.reshape(batch, seq, d_model), hs.reshape(n_seq, 1, d_model),
            *[jnp.stack(o) for o in outs])
```

```python
import functools

import numpy as np
import jax
import jax.numpy as jnp
from jax import lax
from jax.experimental import pallas as pl
from jax.experimental.pallas import tpu as pltpu

F32 = jnp.float32
BF16 = jnp.bfloat16

HEAD_DIM = 128
MOBA_BLOCK = 256
MOBA_TOPK = 3
LN_EPS = 1e-5
RMS_EPS = 1e-6
NEG_INF = -1e30
ATTN_SCALE = HEAD_DIM ** -0.5
SB_TILE = 512
MOBA_KEYS = 4 * MOBA_BLOCK
LOG2E = 1.4426950408889634
SUFFIX_CHUNK = 128
VMEM_LIMIT = 56 * 1024 * 1024


def _params(*sem):
    return pltpu.CompilerParams(dimension_semantics=sem, vmem_limit_bytes=VMEM_LIMIT)


def _dot_nt(a, b):
    return lax.dot_general(a, b, (((1,), (1,)), ((), ())), preferred_element_type=F32)


def _head_rms(o, g):
    return o * lax.rsqrt(jnp.mean(o * o, axis=-1, keepdims=True) + RMS_EPS) * g


def _layer_norm(y, g, b):
    mu = jnp.mean(y, axis=-1, keepdims=True)
    d = y - mu
    var = jnp.mean(d * d, axis=-1, keepdims=True)
    return d * lax.rsqrt(var + LN_EPS) * g + b


def _neg_softplus(z):
    return -(jnp.maximum(z, 0.0) + jnp.log(1.0 + jnp.exp(-jnp.abs(z))))


def _split(x):
    hi = x.astype(BF16)
    return hi, (x - hi.astype(F32)).astype(BF16)


def _suffix_matrix():
    j = np.arange(SUFFIX_CHUNK)
    u = (j[:, None] > j[None, :]).astype(np.float32)
    return jnp.asarray(np.concatenate([u, u], axis=0), dtype=BF16)


def _proj_kernel(x_ref, w_ref, o_ref, xb_ref):
    @pl.when(pl.program_id(1) == 0)
    def _():
        xb_ref[...] = x_ref[...].astype(BF16)

    o_ref[...] = jnp.dot(xb_ref[...], w_ref[...], preferred_element_type=F32)


def _proj(x, w_stack, layer, tm, tn):
    m, k = x.shape
    n = w_stack.shape[2]
    tm, tn = min(tm, m), min(tn, n)
    return pl.pallas_call(
        _proj_kernel,
        grid=(m // tm, n // tn),
        in_specs=[pl.BlockSpec((tm, k), lambda i, j: (i, 0)),
                  pl.BlockSpec((None, k, tn), lambda i, j: (layer, 0, j))],
        out_specs=pl.BlockSpec((tm, tn), lambda i, j: (i, j)),
        out_shape=jax.ShapeDtypeStruct((m, n), F32),
        scratch_shapes=[pltpu.VMEM((tm, k), BF16)],
        compiler_params=_params("parallel", "arbitrary"),
        name="proj",
    )(x, w_stack)


def _out_ln_kernel(x_ref, w_ref, r_ref, g_ref, b_ref, o_ref, acc_ref, *, alpha):
    k = pl.program_id(1)

    @pl.when(k == 0)
    def _():
        acc_ref[...] = jnp.zeros_like(acc_ref)

    acc_ref[...] += jnp.dot(x_ref[...], w_ref[...], preferred_element_type=F32)

    @pl.when(k == pl.num_programs(1) - 1)
    def _():
        o_ref[...] = _layer_norm(alpha * r_ref[...] + acc_ref[...], g_ref[...], b_ref[...])


def _out_ln(x, w_stack, res, g, b, layer, alpha, tm, tk):
    m, kdim = x.shape
    n = w_stack.shape[2]
    tm, tk = min(tm, m), min(tk, kdim)
    return pl.pallas_call(
        functools.partial(_out_ln_kernel, alpha=alpha),
        grid=(m // tm, kdim // tk),
        in_specs=[pl.BlockSpec((tm, tk), lambda i, k: (i, k)),
                  pl.BlockSpec((None, tk, n), lambda i, k: (layer, k, 0)),
                  pl.BlockSpec((tm, n), lambda i, k: (i, 0)),
                  pl.BlockSpec((None, 1, n), lambda i, k: (layer, 0, 0)),
                  pl.BlockSpec((None, 1, n), lambda i, k: (layer, 0, 0))],
        out_specs=pl.BlockSpec((tm, n), lambda i, k: (i, 0)),
        out_shape=jax.ShapeDtypeStruct((m, n), F32),
        scratch_shapes=[pltpu.VMEM((tm, n), F32)],
        compiler_params=_params("parallel", "arbitrary"),
        name="out_ln",
    )(x, w_stack, res, g, b)


def _mlp_kernel(x_ref, wu_ref, wd_ref, g_ref, b_ref, o_ref, acc_ref, xb_ref, *, alpha):
    f = pl.program_id(1)

    @pl.when(f == 0)
    def _():
        xb_ref[...] = x_ref[...].astype(BF16)
        acc_ref[...] = jnp.zeros_like(acc_ref)

    h = jnp.maximum(jnp.dot(xb_ref[...], wu_ref[...], preferred_element_type=F32), 0.0)
    acc_ref[...] += jnp.dot((h * h).astype(BF16), wd_ref[...], preferred_element_type=F32)

    @pl.when(f == pl.num_programs(1) - 1)
    def _():
        o_ref[...] = _layer_norm(alpha * x_ref[...] + acc_ref[...], g_ref[...], b_ref[...])


def _mlp(x, wu_stack, wd_stack, g, b, layer, alpha, tm, tf):
    m, d = x.shape
    ff = wu_stack.shape[2]
    tm, tf = min(tm, m), min(tf, ff)
    return pl.pallas_call(
        functools.partial(_mlp_kernel, alpha=alpha),
        grid=(m // tm, ff // tf),
        in_specs=[pl.BlockSpec((tm, d), lambda i, f: (i, 0)),
                  pl.BlockSpec((None, d, tf), lambda i, f: (layer, 0, f)),
                  pl.BlockSpec((None, tf, d), lambda i, f: (layer, f, 0)),
                  pl.BlockSpec((None, 1, d), lambda i, f: (layer, 0, 0)),
                  pl.BlockSpec((None, 1, d), lambda i, f: (layer, 0, 0))],
        out_specs=pl.BlockSpec((tm, d), lambda i, f: (i, 0)),
        out_shape=jax.ShapeDtypeStruct((m, d), F32),
        scratch_shapes=[pltpu.VMEM((tm, d), F32), pltpu.VMEM((tm, d), BF16)],
        compiler_params=_params("parallel", "arbitrary"),
        name="mlp",
    )(x, wu_stack, wd_stack, g, b)


def _causal_pairs(n_q, q_per_k):
    qi, kj = [], []
    for i in range(n_q):
        for j in range(i // q_per_k, -1, -1):
            qi.append(i)
            kj.append(j)
    return jnp.asarray(qi, jnp.int32), jnp.asarray(kj, jnp.int32)


def _sb_prompt_kernel(qi_ref, kj_ref, q_ref, k_ref, v_ref, uu_ref, g_ref, o_ref,
                      acc_ref, carry_ref, qb_ref):
    p = pl.program_id(2)
    qi, kj = qi_ref[p], kj_ref[p]
    t = q_ref.shape[0]
    chunks = [slice(c * SUFFIX_CHUNK, (c + 1) * SUFFIX_CHUNK) for c in range(t // SUFFIX_CHUNK)]

    @pl.when(kj == qi)
    def _():
        acc_ref[...] = jnp.zeros_like(acc_ref)
        carry_ref[...] = jnp.zeros_like(carry_ref)
        qb_ref[...] = q_ref[...].astype(BF16)

    def tile(diagonal):
        z = _dot_nt(qb_ref[...], k_ref[...].astype(BF16)) * ATTN_SCALE
        nsp = _neg_softplus(z)
        if diagonal:
            past = lax.broadcasted_iota(jnp.int32, (t, t), 1) < lax.broadcasted_iota(jnp.int32, (t, t), 0)
            lk = jnp.where(past, nsp, 0.0)
        else:
            lk = nsp
        hi, lo = _split(lk)
        uu = uu_ref[...]
        locs = [jnp.dot(jnp.concatenate([hi[:, c], lo[:, c]], axis=1), uu, preferred_element_type=F32)
                for c in chunks]
        run = carry_ref[...]
        between = [None] * len(chunks)
        for i in reversed(range(len(chunks))):
            between[i] = locs[i] + run
            run = run + (locs[i][:, 0:1] + lk[:, chunks[i].start:chunks[i].start + 1])
        w = jnp.exp(z + nsp + jnp.concatenate(between, axis=1))
        if diagonal:
            w = jnp.where(past, w, 0.0)
        acc_ref[...] += jnp.dot(w.astype(BF16), v_ref[...].astype(BF16), preferred_element_type=F32)
        carry_ref[...] = run

    @pl.when(kj == qi)
    def _():
        tile(True)

    @pl.when(kj != qi)
    def _():
        tile(False)

    @pl.when(kj == 0)
    def _():
        o_ref[...] = _head_rms(acc_ref[...], g_ref[...]).astype(o_ref.dtype)


def _moba_prompt_kernel(qi_ref, kj_ref, slope_ref, q_ref, k_ref, v_ref, km_ref, g_ref, o_ref,
                        acc_ref, m_ref, l_ref, qb_ref, gate_ref, bias_ref, *, n_top, blocks_per_step):
    h = pl.program_id(1)
    p = pl.program_id(2)
    qi, kj = qi_ref[p], kj_ref[p]
    t = MOBA_BLOCK
    slope2 = slope_ref[h] * LOG2E
    newest = kj == qi // blocks_per_step
    own_slot = qi % blocks_per_step

    @pl.when(newest)
    def _():
        acc_ref[...] = jnp.zeros_like(acc_ref)
        m_ref[...] = jnp.full_like(m_ref, NEG_INF)
        l_ref[...] = jnp.zeros_like(l_ref)
        qb_ref[...] = (q_ref[...] * (ATTN_SCALE * LOG2E)).astype(BF16)
        gate_ref[...] = lax.dot_general(q_ref[...], km_ref[...], (((1,), (1,)), ((), ())),
                                        preferred_element_type=F32, precision=lax.Precision.HIGHEST)
        row = lax.broadcasted_iota(jnp.int32, (t, t), 0)
        col = lax.broadcasted_iota(jnp.int32, (t, t), 1)
        bias_ref[...] = (col - row).astype(F32) * slope2

    def selected(blk):
        gate = gate_ref[...]
        lane = lax.broadcasted_iota(jnp.int32, gate.shape, 1)
        g_j = jnp.sum(jnp.where(lane == blk, gate, 0.0), axis=-1, keepdims=True)
        beats = ((gate > g_j) | ((gate == g_j) & (lane < blk))) & (lane < qi)
        return jnp.sum(jnp.where(beats, 1.0, 0.0), axis=-1, keepdims=True) < n_top

    def update(slots):
        rows = [slice(s * t, (s + 1) * t) for s, _, _ in slots]
        z = _dot_nt(qb_ref[...], jnp.concatenate([k_ref[r, :] for r in rows], axis=0).astype(BF16))
        bias = bias_ref[...]
        parts = []
        for i, (_, blk, kind) in enumerate(slots):
            s = (z[:, i * t:(i + 1) * t] - slope2 * ((qi - blk) * t).astype(F32)) + bias
            if kind == "own":
                mask = lax.broadcasted_iota(jnp.int32, (t, t), 1) <= lax.broadcasted_iota(jnp.int32, (t, t), 0)
            else:
                mask = selected(blk)
            parts.append(jnp.where(mask, s, NEG_INF))
        m_old = m_ref[...]
        m_new = m_old
        for s in parts:
            m_new = jnp.maximum(m_new, jnp.max(s, axis=-1, keepdims=True))
        a = jnp.exp2(m_old - m_new)
        es = [jnp.exp2(s - m_new) for s in parts]
        l_new = a * l_ref[...]
        for e in es:
            l_new = l_new + jnp.sum(e, axis=-1, keepdims=True)
        e_all = jnp.concatenate(es, axis=1).astype(BF16)
        v_all = jnp.concatenate([v_ref[r, :] for r in rows], axis=0).astype(BF16)
        acc_ref[...] = a * acc_ref[...] + jnp.dot(e_all, v_all, preferred_element_type=F32)
        l_ref[...] = l_new
        m_ref[...] = m_new

    for r in range(blocks_per_step):
        @pl.when(newest & (own_slot == r))
        def _(r=r):
            update([(r, qi, "own")] + [(i, qi - r + i, "past") for i in range(r)])

    @pl.when(jnp.logical_not(newest))
    def _():
        update([(i, blocks_per_step * kj + i, "past") for i in range(blocks_per_step)])

    @pl.when(kj == 0)
    def _():
        o_ref[...] = _head_rms(acc_ref[...] / l_ref[...], g_ref[...]).astype(o_ref.dtype)


def _mem_prompt_kernel(q_ref, k_ref, v_ref, g_ref, o_ref):
    s = _dot_nt(q_ref[...].astype(BF16), k_ref[...].astype(BF16)) * ATTN_SCALE
    e = jnp.exp(s - jnp.max(s, axis=-1, keepdims=True))
    o = jnp.dot(e.astype(BF16), v_ref[...].astype(BF16), preferred_element_type=F32)
    o = o / jnp.sum(e, axis=-1, keepdims=True)
    o_ref[...] = _head_rms(o, g_ref[...]).astype(o_ref.dtype)


def _block_mean_kernel(k_ref, o_ref, *, group):
    x = k_ref[...]
    o_ref[...] = jnp.sum(x.reshape(group, MOBA_BLOCK, x.shape[-1]), axis=1) * (1.0 / MOBA_BLOCK)


def _prompt_attention(proj, memkv, gain, slopes, uu, layer, batch, seq, h_sb, h_mb, h_mem, mem_len):
    rows = batch * seq
    hd = HEAD_DIM
    c_qsb, c_ksb, c_vsb = 0, h_sb, 2 * h_sb
    c_qmb, c_kmb, c_vmb = 3 * h_sb, 3 * h_sb + h_mb, 3 * h_sb + 2 * h_mb
    c_qmem = 3 * h_sb + 3 * h_mb
    assert c_kmb % h_mb == 0

    def tile_map(tab, n_tiles, off):
        return lambda b, h, p, qi, kj: (b * n_tiles + (qi, kj)[tab][p], off + h)

    def gmap(off):
        return lambda b, h, p, qi, kj: (layer, 0, off + h)

    t = min(SB_TILE, seq)
    n_t = seq // t
    qi_tab, kj_tab = _causal_pairs(n_t, 1)
    o_sb = pl.pallas_call(
        _sb_prompt_kernel,
        grid_spec=pltpu.PrefetchScalarGridSpec(
            num_scalar_prefetch=2, grid=(batch, h_sb, int(qi_tab.shape[0])),
            in_specs=[pl.BlockSpec((t, hd), tile_map(0, n_t, c_qsb)),
                      pl.BlockSpec((t, hd), tile_map(1, n_t, c_ksb)),
                      pl.BlockSpec((t, hd), tile_map(1, n_t, c_vsb)),
                      pl.BlockSpec((2 * SUFFIX_CHUNK, SUFFIX_CHUNK), lambda b, h, p, qi, kj: (0, 0)),
                      pl.BlockSpec((None, 1, hd), gmap(0))],
            out_specs=pl.BlockSpec((t, hd), tile_map(0, n_t, 0)),
            scratch_shapes=[pltpu.VMEM((t, hd), F32), pltpu.VMEM((t, 1), F32), pltpu.VMEM((t, hd), BF16)]),
        out_shape=jax.ShapeDtypeStruct((rows, h_sb * hd), BF16),
        compiler_params=_params("parallel", "parallel", "arbitrary"),
        name="prompt_sb_attn",
    )(qi_tab, kj_tab, proj, proj, proj, uu, gain)

    nblk = seq // MOBA_BLOCK
    group = min(8, batch * nblk)
    kmean = pl.pallas_call(
        functools.partial(_block_mean_kernel, group=group),
        grid=(batch * nblk // group,),
        in_specs=[pl.BlockSpec((group * MOBA_BLOCK, h_mb * hd), lambda i: (i, c_kmb // h_mb))],
        out_specs=pl.BlockSpec((group, h_mb * hd), lambda i: (i, 0)),
        out_shape=jax.ShapeDtypeStruct((batch * nblk, h_mb * hd), F32),
        compiler_params=_params("parallel"),
        name="moba_block_mean",
    )(proj)
    kmean = kmean.reshape(batch, nblk, h_mb, hd).transpose(0, 2, 1, 3)
    kmean = jnp.pad(kmean, ((0, 0), (0, 0), (0, hd - nblk), (0, 0)))

    n_kt = seq // MOBA_KEYS
    blocks_per_step = MOBA_KEYS // MOBA_BLOCK
    qi_tab, kj_tab = _causal_pairs(nblk, blocks_per_step)
    o_mb = pl.pallas_call(
        functools.partial(_moba_prompt_kernel, n_top=min(MOBA_TOPK, nblk), blocks_per_step=blocks_per_step),
        grid_spec=pltpu.PrefetchScalarGridSpec(
            num_scalar_prefetch=2, grid=(batch, h_mb, int(qi_tab.shape[0])),
            in_specs=[pl.BlockSpec(memory_space=pltpu.SMEM),
                      pl.BlockSpec((MOBA_BLOCK, hd), tile_map(0, nblk, c_qmb)),
                      pl.BlockSpec((MOBA_KEYS, hd), tile_map(1, n_kt, c_kmb)),
                      pl.BlockSpec((MOBA_KEYS, hd), tile_map(1, n_kt, c_vmb)),
                      pl.BlockSpec((None, None, hd, hd), lambda b, h, p, qi, kj: (b, h, 0, 0)),
                      pl.BlockSpec((None, 1, hd), gmap(h_sb))],
            out_specs=pl.BlockSpec((MOBA_BLOCK, hd), tile_map(0, nblk, 0)),
            scratch_shapes=[pltpu.VMEM((MOBA_BLOCK, hd), F32), pltpu.VMEM((MOBA_BLOCK, 1), F32),
                            pltpu.VMEM((MOBA_BLOCK, 1), F32), pltpu.VMEM((MOBA_BLOCK, hd), BF16),
                            pltpu.VMEM((MOBA_BLOCK, hd), F32), pltpu.VMEM((MOBA_BLOCK, MOBA_BLOCK), F32)]),
        out_shape=jax.ShapeDtypeStruct((rows, h_mb * hd), BF16),
        compiler_params=_params("parallel", "parallel", "arbitrary"),
        name="prompt_moba_attn",
    )(qi_tab, kj_tab, slopes, proj, proj, proj, kmean, gain)

    tq = min(512, seq)
    nq = seq // tq
    o_mem = pl.pallas_call(
        _mem_prompt_kernel,
        grid=(batch, h_mem, nq),
        in_specs=[pl.BlockSpec((tq, hd), lambda b, h, i: (b * nq + i, c_qmem + h)),
                  pl.BlockSpec((mem_len, hd), lambda b, h, i: (b, h)),
                  pl.BlockSpec((mem_len, hd), lambda b, h, i: (b, h_mem + h)),
                  pl.BlockSpec((None, 1, hd), lambda b, h, i: (layer, 0, h_sb + h_mb + h))],
        out_specs=pl.BlockSpec((tq, hd), lambda b, h, i: (b * nq + i, h)),
        out_shape=jax.ShapeDtypeStruct((rows, h_mem * hd), BF16),
        compiler_params=_params("parallel", "parallel", "parallel"),
        name="prompt_mem_attn",
    )(proj, memkv, memkv, gain)
    return jnp.concatenate([o_sb, o_mb, o_mem], axis=1)


def _kv_rows_kernel(*refs, n):
    for src_ref, dst_ref in zip(refs[:n], refs[2 * n:]):
        dst_ref[...] = src_ref[...]


def _store_kv_rows(proj, bufs, col_blocks, layer, batch, seq, n_heads):
    n = len(bufs)
    hd = HEAD_DIM
    tt = min(1024, seq)
    n_t = seq // tt
    return pl.pallas_call(
        functools.partial(_kv_rows_kernel, n=n),
        grid=(batch, n_heads, n_t),
        in_specs=[pl.BlockSpec((tt, hd), functools.partial(lambda b, h, i, c: (b * n_t + i, c + h), c=c))
                  for c in col_blocks] + [pl.BlockSpec(memory_space=pl.ANY)] * n,
        out_specs=[pl.BlockSpec((None, None, None, tt, hd), lambda b, h, i: (layer, b, h, i, 0))] * n,
        out_shape=[jax.ShapeDtypeStruct(buf.shape, buf.dtype) for buf in bufs],
        input_output_aliases={n + k: k for k in range(n)},
        compiler_params=_params("parallel", "parallel", "parallel"),
        name="store_kv_rows",
    )(*([proj] * n), *bufs)


def _dot3(x, w):
    xh, xl = _split(x)
    wh, wl = _split(w)
    m = x.shape[0]
    a = jnp.dot(jnp.concatenate([xh, xl], axis=0), wh, preferred_element_type=F32)
    return a[:m] + a[m:] + jnp.dot(xh, wl, preferred_element_type=F32)


def _dot3_nt(x, k):
    xh, xl = _split(x)
    kh, kl = _split(k)
    m = x.shape[0]
    a = _dot_nt(jnp.concatenate([xh, xl], axis=0), kh)
    return a[:m] + a[m:] + _dot_nt(xh, kl)


def _proj3_kernel(x_ref, w_ref, o_ref):
    o_ref[...] = _dot3(x_ref[...], w_ref[...])


def _proj3(x, w_stack, layer, tn):
    m, k = x.shape
    n = w_stack.shape[2]
    tn = min(tn, n)
    return pl.pallas_call(
        _proj3_kernel,
        grid=(n // tn,),
        in_specs=[pl.BlockSpec((m, k), lambda j: (0, 0)),
                  pl.BlockSpec((None, k, tn), lambda j: (layer, 0, j))],
        out_specs=pl.BlockSpec((m, tn), lambda j: (0, j)),
        out_shape=jax.ShapeDtypeStruct((m, n), F32),
        compiler_params=_params("parallel"),
        name="sample_proj",
    )(x, w_stack)


def _out_ln3_kernel(x_ref, w_ref, r_ref, g_ref, b_ref, o_ref, acc_ref, *, alpha):
    k = pl.program_id(0)

    @pl.when(k == 0)
    def _():
        acc_ref[...] = jnp.zeros_like(acc_ref)

    acc_ref[...] += _dot3(x_ref[...], w_ref[...])

    @pl.when(k == pl.num_programs(0) - 1)
    def _():
        o_ref[...] = _layer_norm(alpha * r_ref[...] + acc_ref[...], g_ref[...], b_ref[...])


def _out_ln3(x, w_stack, res, g, b, layer, alpha, tk):
    m, kdim = x.shape
    n = w_stack.shape[2]
    tk = min(tk, kdim)
    return pl.pallas_call(
        functools.partial(_out_ln3_kernel, alpha=alpha),
        grid=(kdim // tk,),
        in_specs=[pl.BlockSpec((m, tk), lambda k: (0, k)),
                  pl.BlockSpec((None, tk, n), lambda k: (layer, k, 0)),
                  pl.BlockSpec((m, n), lambda k: (0, 0)),
                  pl.BlockSpec((None, 1, n), lambda k: (layer, 0, 0)),
                  pl.BlockSpec((None, 1, n), lambda k: (layer, 0, 0))],
        out_specs=pl.BlockSpec((m, n), lambda k: (0, 0)),
        out_shape=jax.ShapeDtypeStruct((m, n), F32),
        scratch_shapes=[pltpu.VMEM((m, n), F32)],
        compiler_params=_params("arbitrary"),
        name="sample_out_ln",
    )(x, w_stack, res, g, b)


def _mlp3_kernel(x_ref, wu_ref, wd_ref, g_ref, b_ref, o_ref, acc_ref, *, alpha):
    f = pl.program_id(0)

    @pl.when(f == 0)
    def _():
        acc_ref[...] = jnp.zeros_like(acc_ref)

    h = jnp.maximum(_dot3(x_ref[...], wu_ref[...]), 0.0)
    acc_ref[...] += _dot3(h * h, wd_ref[...])

    @pl.when(f == pl.num_programs(0) - 1)
    def _():
        o_ref[...] = _layer_norm(alpha * x_ref[...] + acc_ref[...], g_ref[...], b_ref[...])


def _mlp3(x, wu_stack, wd_stack, g, b, layer, alpha, tf):
    m, d = x.shape
    ff = wu_stack.shape[2]
    tf = min(tf, ff)
    return pl.pallas_call(
        functools.partial(_mlp3_kernel, alpha=alpha),
        grid=(ff // tf,),
        in_specs=[pl.BlockSpec((m, d), lambda f: (0, 0)),
                  pl.BlockSpec((None, d, tf), lambda f: (layer, 0, f)),
                  pl.BlockSpec((None, tf, d), lambda f: (layer, f, 0)),
                  pl.BlockSpec((None, 1, d), lambda f: (layer, 0, 0)),
                  pl.BlockSpec((None, 1, d), lambda f: (layer, 0, 0))],
        out_specs=pl.BlockSpec((m, d), lambda f: (0, 0)),
        out_shape=jax.ShapeDtypeStruct((m, d), F32),
        scratch_shapes=[pltpu.VMEM((m, d), F32)],
        compiler_params=_params("arbitrary"),
        name="sample_mlp",
    )(x, wu_stack, wd_stack, g, b)


def _pick_rows(parts):
    row = lax.broadcasted_iota(jnp.int32, parts[0].shape, 0)
    out = jnp.zeros_like(parts[0])
    for h, part in enumerate(parts):
        out = jnp.where(row == h, part, out)
    return out


def _own_head_blocks(x, n_heads):
    return _pick_rows([x[:, h * HEAD_DIM:(h + 1) * HEAD_DIM] for h in range(n_heads)])


def _head_scores(q, keys):
    row = lax.broadcasted_iota(jnp.int32, q.shape, 0)
    q_bd = jnp.concatenate([jnp.where(row == h, q, 0.0) for h in range(len(keys))], axis=1)
    return _dot3_nt(q_bd, jnp.concatenate(keys, axis=1)) * ATTN_SCALE


def _head_values(w, values):
    return _dot3(w, jnp.concatenate(values, axis=1))


def _suffix_sums3(lk, uu):
    hi, lo = _split(lk)
    lo2 = (lk - hi.astype(F32) - lo.astype(F32)).astype(BF16)
    loc = jnp.dot(jnp.concatenate([hi, lo], axis=1), uu, preferred_element_type=F32)
    loc = loc + jnp.dot(lo2, uu[:SUFFIX_CHUNK], preferred_element_type=F32)
    return loc, loc[:, 0:1] + lk[:, 0:1]


def _sb_decode_kernel(pt_ref, q_ref, *refs, n_heads, n_sub):
    k_refs, v_refs = refs[:n_sub], refs[n_sub:2 * n_sub]
    uu_ref, g_ref, o_ref, acc_ref, carry_ref = refs[2 * n_sub:]
    j = pl.program_id(1)

    @pl.when(j == 0)
    def _():
        acc_ref[...] = jnp.zeros_like(acc_ref)
        carry_ref[...] = jnp.zeros_like(carry_ref)

    def tokens(page_refs, h):
        return jnp.concatenate([r[h] for r in page_refs], axis=0)

    z = _head_scores(q_ref[...], [tokens(k_refs, h) for h in range(n_heads)])
    nsp = _neg_softplus(z)
    uu = uu_ref[...]
    run = carry_ref[...]
    n_chunks = z.shape[1] // SUFFIX_CHUNK
    between = [None] * n_chunks
    for c in reversed(range(n_chunks)):
        loc, tot = _suffix_sums3(nsp[:, c * SUFFIX_CHUNK:(c + 1) * SUFFIX_CHUNK], uu)
        between[c] = loc + run
        run = run + tot
    w = jnp.exp(z + nsp + jnp.concatenate(between, axis=1))
    acc_ref[...] += _head_values(w, [tokens(v_refs, h) for h in range(n_heads)])
    carry_ref[...] = run

    @pl.when(j == pl.num_programs(1) - 1)
    def _():
        o_ref[...] = _head_rms(_own_head_blocks(acc_ref[...], n_heads), g_ref[...])


def _moba_gate_kernel(pt_ref, q_ref, *refs, n_top, n_sub, pages_per_block):
    k_refs = refs[:n_sub * pages_per_block]
    idx_ref, gate_ref = refs[n_sub * pages_per_block:]
    n = pl.program_id(1)

    @pl.when(n == 0)
    def _():
        gate_ref[...] = jnp.full_like(gate_ref, NEG_INF)

    n_heads = k_refs[0].shape[0]
    lane = lax.broadcasted_iota(jnp.int32, (n_heads, HEAD_DIM), 1)
    gate = gate_ref[0:n_heads, :]
    for b in range(n_sub):
        pages = k_refs[b * pages_per_block:(b + 1) * pages_per_block]
        mean = sum(jnp.sum(r[...], axis=1) for r in pages) * (1.0 / MOBA_BLOCK)
        g = jnp.sum(q_ref[0:n_heads, :] * mean, axis=-1, keepdims=True)
        gate = jnp.where(lane == n * n_sub + b, g, gate)
    gate_ref[0:n_heads, :] = gate

    @pl.when(n == pl.num_programs(1) - 1)
    def _():
        gate = gate_ref[...]
        lanes = lax.broadcasted_iota(jnp.int32, gate.shape, 1)
        idx = jnp.zeros(gate.shape, jnp.int32)
        for r in range(n_top):
            best = jnp.max(gate, axis=-1, keepdims=True)
            first = jnp.min(jnp.where(gate == best, lanes, HEAD_DIM), axis=-1, keepdims=True)
            idx = jnp.where(lanes == r, first, idx)
            gate = jnp.where(lanes == first, NEG_INF, gate)
        idx_ref[...] = idx


def _moba_decode_kernel(pt_ref, top_ref, slope_ref, q_ref, kn_ref, vn_ref, g_ref, *refs, n_heads, past_len, page):
    k_refs, v_refs = refs[:n_heads], refs[n_heads:2 * n_heads]
    o_ref, acc_ref, m_ref, l_ref = refs[2 * n_heads:]
    s_id, t = pl.program_id(0), pl.program_id(1)
    pages_per_block = MOBA_BLOCK // page
    q = q_ref[...]

    @pl.when(t == 0)
    def _():
        m_ref[...] = jnp.sum(q * kn_ref[...], axis=-1, keepdims=True) * ATTN_SCALE
        l_ref[...] = jnp.ones_like(l_ref)
        acc_ref[...] = vn_ref[...]

    row = lax.broadcasted_iota(jnp.int32, (8, 1), 0)
    slope = jnp.zeros((8, 1), F32)
    dist0 = jnp.zeros((8, 1), F32)
    for h in range(n_heads):
        kpos0 = top_ref[s_id, h, t // pages_per_block] * MOBA_BLOCK + (t % pages_per_block) * page
        slope = jnp.where(row == h, slope_ref[h], slope)
        dist0 = jnp.where(row == h, (past_len - kpos0).astype(F32), dist0)
    lane = lax.broadcasted_iota(jnp.int32, (8, page), 1)
    s = _head_scores(q, [r[...] for r in k_refs]) - slope * (dist0 - lane.astype(F32))
    m_old = m_ref[...]
    m_new = jnp.maximum(m_old, jnp.max(s, axis=-1, keepdims=True))
    a = jnp.exp(m_old - m_new)
    e = jnp.exp(s - m_new)
    l_ref[...] = a * l_ref[...] + jnp.sum(e, axis=-1, keepdims=True)
    acc_ref[...] = a * acc_ref[...] + _own_head_blocks(_head_values(e, [r[...] for r in v_refs]), n_heads)
    m_ref[...] = m_new

    @pl.when(t == pl.num_programs(1) - 1)
    def _():
        o_ref[...] = _head_rms(acc_ref[...] / l_ref[...], g_ref[...])


def _mem_decode_kernel(q_ref, k_ref, v_ref, g_ref, o_ref, *, n_heads):
    s = _head_scores(q_ref[...], [k_ref[:, h, :] for h in range(n_heads)])
    e = jnp.exp(s - jnp.max(s, axis=-1, keepdims=True))
    o = _head_values(e, [v_ref[:, h, :] for h in range(n_heads)])
    o = _own_head_blocks(o, n_heads) / jnp.sum(e, axis=-1, keepdims=True)
    o_ref[...] = _head_rms(o, g_ref[...])


def _heads_to_rows(x, n_heads):
    x = x.reshape(x.shape[0], n_heads, HEAD_DIM)
    return jnp.pad(x, ((0, 0), (0, 8 - n_heads), (0, 0)))


def _pages_per_step(n, most):
    return max(d for d in range(1, most + 1) if n % d == 0)


def _sample_attention(proj, gain_l, page_table, pools, mem_k, mem_v, slopes, uu, layer, h_sb, h_mb, h_mem):
    cache_k_sb, cache_v_sb, cache_k_mb, cache_v_mb = pools
    n_seq, n_pages = page_table.shape
    page = cache_k_sb.shape[3]
    past_len = n_pages * page
    hd = HEAD_DIM
    w_sb, w_mb = h_sb * hd, h_mb * hd
    q_sb = _heads_to_rows(proj[:, 0:w_sb], h_sb)
    q_mb = _heads_to_rows(proj[:, 3 * w_sb:3 * w_sb + w_mb], h_mb)
    k_mb_new = _heads_to_rows(proj[:, 3 * w_sb + w_mb:3 * w_sb + 2 * w_mb], h_mb)
    v_mb_new = _heads_to_rows(proj[:, 3 * w_sb + 2 * w_mb:3 * w_sb + 3 * w_mb], h_mb)
    q_mem = _heads_to_rows(proj[:, 3 * w_sb + 3 * w_mb:], h_mem)
    g_sb = _heads_to_rows(gain_l[None, 0:w_sb], h_sb)[0]
    g_mb = _heads_to_rows(gain_l[None, w_sb:w_sb + w_mb], h_mb)[0]
    g_mem = _heads_to_rows(gain_l[None, w_sb + w_mb:], h_mem)[0]

    row_spec = pl.BlockSpec((None, 8, hd), lambda s, j, *_: (s, 0, 0))
    gain_spec = pl.BlockSpec((8, hd), lambda s, j, *_: (0, 0))
    row_shape = jax.ShapeDtypeStruct((n_seq, 8, hd), F32)

    def page_spec(n_heads, page_of):
        return pl.BlockSpec((None, None, n_heads, page, hd),
                            lambda s, j, pt: (layer, pt[s, page_of(j)], 0, 0, 0))

    n_sub = _pages_per_step(n_pages, 4)
    sb_pages = [page_spec(h_sb, functools.partial(lambda j, i: n_pages - n_sub * (j + 1) + i, i=i))
                for i in range(n_sub)]
    o_sb = pl.pallas_call(
        functools.partial(_sb_decode_kernel, n_heads=h_sb, n_sub=n_sub),
        grid_spec=pltpu.PrefetchScalarGridSpec(
            num_scalar_prefetch=1, grid=(n_seq, n_pages // n_sub),
            in_specs=[row_spec] + sb_pages + sb_pages
                     + [pl.BlockSpec((2 * SUFFIX_CHUNK, SUFFIX_CHUNK), lambda s, j, pt: (0, 0)), gain_spec],
            out_specs=row_spec,
            scratch_shapes=[pltpu.VMEM((8, h_sb * hd), F32), pltpu.VMEM((8, 1), F32)]),
        out_shape=row_shape,
        compiler_params=_params("parallel", "arbitrary"),
        name="sample_sb_attn",
    )(page_table, q_sb, *([cache_k_sb] * n_sub), *([cache_v_sb] * n_sub), uu, g_sb)

    pages_per_block = MOBA_BLOCK // page
    n_past_blocks = past_len // MOBA_BLOCK
    n_top = min(MOBA_TOPK, n_past_blocks)
    n_gate = _pages_per_step(n_past_blocks, 4)
    gate_pages = [page_spec(h_mb, functools.partial(lambda n, i: n_gate * pages_per_block * n + i, i=i))
                  for i in range(n_gate * pages_per_block)]
    top, _ = pl.pallas_call(
        functools.partial(_moba_gate_kernel, n_top=n_top, n_sub=n_gate, pages_per_block=pages_per_block),
        grid_spec=pltpu.PrefetchScalarGridSpec(
            num_scalar_prefetch=1, grid=(n_seq, n_past_blocks // n_gate),
            in_specs=[row_spec] + gate_pages,
            out_specs=[row_spec, row_spec]),
        out_shape=[jax.ShapeDtypeStruct((n_seq, 8, hd), jnp.int32), row_shape],
        compiler_params=_params("parallel", "arbitrary"),
        name="sample_moba_gate",
    )(page_table, q_mb, *([cache_k_mb] * (n_gate * pages_per_block)))
    top = top[:, :, :max(n_top, 1)]

    def sel_page(h):
        return pl.BlockSpec(
            (None, None, None, page, hd),
            lambda s, t, pt, tp: (layer, pt[s, tp[s, h, t // pages_per_block] * pages_per_block + t % pages_per_block],
                                  h, 0, 0))

    sel_pages = [sel_page(h) for h in range(h_mb)]
    o_mb = pl.pallas_call(
        functools.partial(_moba_decode_kernel, n_heads=h_mb, past_len=past_len, page=page),
        grid_spec=pltpu.PrefetchScalarGridSpec(
            num_scalar_prefetch=2, grid=(n_seq, n_top * pages_per_block),
            in_specs=[pl.BlockSpec(memory_space=pltpu.SMEM), row_spec, row_spec, row_spec, gain_spec]
                     + sel_pages + sel_pages,
            out_specs=row_spec,
            scratch_shapes=[pltpu.VMEM((8, hd), F32), pltpu.VMEM((8, 1), F32), pltpu.VMEM((8, 1), F32)]),
        out_shape=row_shape,
        compiler_params=_params("parallel", "arbitrary"),
        name="sample_moba_attn",
    )(page_table, top, slopes, q_mb, k_mb_new, v_mb_new, g_mb, *([cache_k_mb] * h_mb), *([cache_v_mb] * h_mb))

    mem_len = mem_k.shape[2]
    mem_spec = pl.BlockSpec((None, None, mem_len, h_mem, hd), lambda s: (layer, s, 0, 0, 0))
    o_mem = pl.pallas_call(
        functools.partial(_mem_decode_kernel, n_heads=h_mem),
        grid=(n_seq,),
        in_specs=[pl.BlockSpec((None, 8, hd), lambda s: (s, 0, 0)), mem_spec, mem_spec,
                  pl.BlockSpec((8, hd), lambda s: (0, 0))],
        out_specs=pl.BlockSpec((None, 8, hd), lambda s: (s, 0, 0)),
        out_shape=row_shape,
        compiler_params=_params("parallel"),
        name="sample_mem_attn",
    )(q_mem, mem_k, mem_v, g_mem)

    return jnp.concatenate([o_sb[:, :h_sb].reshape(n_seq, w_sb), o_mb[:, :h_mb].reshape(n_seq, w_mb),
                            o_mem[:, :h_mem].reshape(n_seq, h_mem * hd)], axis=1)


def kernel(x_prompt, x_sample, mem_prompt, cache_k_sb, cache_v_sb, cache_k_moba, cache_v_moba, cache_mem_k, cache_mem_v, page_table, w_in, w_mem_kv, out_norm_g, w_o, ln1_g, ln1_b, w_up, w_down, ln2_g, ln2_b):
    batch, seq, d_model = x_prompt.shape
    n_seq, dec_seq, _ = x_sample.shape
    depth = w_in.shape[0]
    h_sb, h_mb, h_mem = cache_k_sb.shape[3], cache_k_moba.shape[3], cache_mem_k.shape[3]
    mem_len = mem_prompt.shape[1]
    hd = HEAD_DIM
    assert dec_seq == 1 and cache_k_sb.shape[4] == hd and seq % MOBA_KEYS == 0
    assert MOBA_BLOCK % cache_k_sb.shape[2] == 0 and seq // MOBA_BLOCK <= hd
    w_mem = h_mem * hd
    alpha = (2 * depth) ** 0.25

    w_in_b, w_kv_b, w_o_b = w_in.astype(BF16), w_mem_kv.astype(BF16), w_o.astype(BF16)
    w_up_b, w_down_b = w_up.astype(BF16), w_down.astype(BF16)
    gain = out_norm_g.reshape(depth, 1, d_model)
    ln1_g3, ln1_b3 = ln1_g.reshape(depth, 1, d_model), ln1_b.reshape(depth, 1, d_model)
    ln2_g3, ln2_b3 = ln2_g.reshape(depth, 1, d_model), ln2_b.reshape(depth, 1, d_model)
    slopes = jnp.exp2(-8.0 * jnp.arange(1, h_mb + 1, dtype=F32) / h_mb)
    uu = _suffix_matrix()
    pools = tuple(jnp.transpose(c, (0, 1, 3, 2, 4)) for c in (cache_k_sb, cache_v_sb, cache_k_moba, cache_v_moba))

    hp = x_prompt.reshape(batch * seq, d_model)
    hs = x_sample.reshape(n_seq, d_model)
    mem2d = mem_prompt.reshape(batch * mem_len, d_model)
    assert h_sb == h_mb
    kv_p = [jnp.zeros((depth, batch, h_sb, seq, hd), F32) for _ in range(4)]
    kv_cols = (h_sb, 2 * h_sb, 3 * h_sb + h_mb, 3 * h_sb + 2 * h_mb)
    outs = [[] for _ in range(10)]
    for l in range(depth):
        memkv = _proj(mem2d, w_kv_b, l, 512, 512)
        proj_p = _proj(hp, w_in_b, l, 1024, 1024)
        mix_p = _prompt_attention(proj_p, memkv, gain, slopes, uu, l, batch, seq, h_sb, h_mb, h_mem, mem_len)
        hp = _out_ln(mix_p, w_o_b, hp, ln1_g3, ln1_b3, l, alpha, 1024, 512)
        hp = _mlp(hp, w_up_b, w_down_b, ln2_g3, ln2_b3, l, alpha, 512, 512)

        proj_s = _proj3(hs, w_in, l, 512)
        mix_s = _sample_attention(proj_s, out_norm_g[l], page_table, pools, cache_mem_k, cache_mem_v,
                                  slopes, uu, l, h_sb, h_mb, h_mem)
        hs = _out_ln3(mix_s, w_o, hs, ln1_g3, ln1_b3, l, alpha, 512)
        hs = _mlp3(hs, w_up, w_down, ln2_g3, ln2_b3, l, alpha, 512)

        kv_p = _store_kv_rows(proj_p, kv_p, kv_cols, l, batch, seq, h_sb)
        for i, c in enumerate(kv_cols):
            outs[6 + i].append(proj_s[:, c * hd:(c + h_sb) * hd].reshape(n_seq, 1, h_sb, hd))
        outs[4].append(memkv[:, :w_mem].reshape(batch, mem_len, h_mem, hd))
        outs[5].append(memkv[:, w_mem:].reshape(batch, mem_len, h_mem, hd))
    return (hp.reshape(batch, seq, d_model), hs.reshape(n_seq, 1, d_model),
            *[jnp.transpose(buf, (0, 1, 3, 2, 4)) for buf in kv_p],
            *[jnp.stack(o) for o in outs[4:]])
```

```python
import functools

import numpy as np
import jax
import jax.numpy as jnp
from jax import lax
from jax.experimental import pallas as pl
from jax.experimental.pallas import tpu as pltpu

F32 = jnp.float32
BF16 = jnp.bfloat16

HEAD_DIM = 128
MOBA_BLOCK = 256
MOBA_TOPK = 3
LN_EPS = 1e-5
RMS_EPS = 1e-6
NEG_INF = -1e30
ATTN_SCALE = HEAD_DIM ** -0.5
SB_TILE = 512
MOBA_KEYS = 4 * MOBA_BLOCK
LOG2E = 1.4426950408889634
SUFFIX_CHUNK = 128
VMEM_LIMIT = 56 * 1024 * 1024


def _params(*sem):
    return pltpu.CompilerParams(dimension_semantics=sem, vmem_limit_bytes=VMEM_LIMIT)


def _dot_nt(a, b):
    return lax.dot_general(a, b, (((1,), (1,)), ((), ())), preferred_element_type=F32)


def _head_rms(o, g):
    return o * lax.rsqrt(jnp.mean(o * o, axis=-1, keepdims=True) + RMS_EPS) * g


def _layer_norm(y, g, b):
    mu = jnp.mean(y, axis=-1, keepdims=True)
    d = y - mu
    var = jnp.mean(d * d, axis=-1, keepdims=True)
    return d * lax.rsqrt(var + LN_EPS) * g + b


def _neg_softplus(z):
    return -(jnp.maximum(z, 0.0) + jnp.log(1.0 + jnp.exp(-jnp.abs(z))))


def _split(x):
    hi = x.astype(BF16)
    return hi, (x - hi.astype(F32)).astype(BF16)


def _suffix_matrix():
    j = np.arange(SUFFIX_CHUNK)
    u = (j[:, None] > j[None, :]).astype(np.float32)
    return jnp.asarray(np.concatenate([u, u], axis=0), dtype=BF16)


def _proj_kernel(x_ref, w_ref, o_ref, xb_ref):
    @pl.when(pl.program_id(1) == 0)
    def _():
        xb_ref[...] = x_ref[...].astype(BF16)

    o_ref[...] = jnp.dot(xb_ref[...], w_ref[...], preferred_element_type=F32)


def _proj(x, w_stack, layer, tm, tn):
    m, k = x.shape
    n = w_stack.shape[2]
    tm, tn = min(tm, m), min(tn, n)
    return pl.pallas_call(
        _proj_kernel,
        grid=(m // tm, n // tn),
        in_specs=[pl.BlockSpec((tm, k), lambda i, j: (i, 0)),
                  pl.BlockSpec((None, k, tn), lambda i, j: (layer, 0, j))],
        out_specs=pl.BlockSpec((tm, tn), lambda i, j: (i, j)),
        out_shape=jax.ShapeDtypeStruct((m, n), F32),
        scratch_shapes=[pltpu.VMEM((tm, k), BF16)],
        compiler_params=_params("parallel", "arbitrary"),
        name="proj",
    )(x, w_stack)


def _out_ln_kernel(x_ref, w_ref, r_ref, g_ref, b_ref, o_ref, acc_ref, *, alpha):
    k = pl.program_id(1)

    @pl.when(k == 0)
    def _():
        acc_ref[...] = jnp.zeros_like(acc_ref)

    acc_ref[...] += jnp.dot(x_ref[...], w_ref[...], preferred_element_type=F32)

    @pl.when(k == pl.num_programs(1) - 1)
    def _():
        o_ref[...] = _layer_norm(alpha * r_ref[...] + acc_ref[...], g_ref[...], b_ref[...])


def _out_ln(x, w_stack, res, g, b, layer, alpha, tm, tk):
    m, kdim = x.shape
    n = w_stack.shape[2]
    tm, tk = min(tm, m), min(tk, kdim)
    return pl.pallas_call(
        functools.partial(_out_ln_kernel, alpha=alpha),
        grid=(m // tm, kdim // tk),
        in_specs=[pl.BlockSpec((tm, tk), lambda i, k: (i, k)),
                  pl.BlockSpec((None, tk, n), lambda i, k: (layer, k, 0)),
                  pl.BlockSpec((tm, n), lambda i, k: (i, 0)),
                  pl.BlockSpec((None, 1, n), lambda i, k: (layer, 0, 0)),
                  pl.BlockSpec((None, 1, n), lambda i, k: (layer, 0, 0))],
        out_specs=pl.BlockSpec((tm, n), lambda i, k: (i, 0)),
        out_shape=jax.ShapeDtypeStruct((m, n), F32),
        scratch_shapes=[pltpu.VMEM((tm, n), F32)],
        compiler_params=_params("parallel", "arbitrary"),
        name="out_ln",
    )(x, w_stack, res, g, b)


def _mlp_kernel(x_ref, wu_ref, wd_ref, g_ref, b_ref, o_ref, acc_ref, xb_ref, *, alpha):
    f = pl.program_id(1)

    @pl.when(f == 0)
    def _():
        xb_ref[...] = x_ref[...].astype(BF16)
        acc_ref[...] = jnp.zeros_like(acc_ref)

    h = jnp.maximum(jnp.dot(xb_ref[...], wu_ref[...], preferred_element_type=F32), 0.0)
    acc_ref[...] += jnp.dot((h * h).astype(BF16), wd_ref[...], preferred_element_type=F32)

    @pl.when(f == pl.num_programs(1) - 1)
    def _():
        o_ref[...] = _layer_norm(alpha * x_ref[...] + acc_ref[...], g_ref[...], b_ref[...])


def _mlp(x, wu_stack, wd_stack, g, b, layer, alpha, tm, tf):
    m, d = x.shape
    ff = wu_stack.shape[2]
    tm, tf = min(tm, m), min(tf, ff)
    return pl.pallas_call(
        functools.partial(_mlp_kernel, alpha=alpha),
        grid=(m // tm, ff // tf),
        in_specs=[pl.BlockSpec((tm, d), lambda i, f: (i, 0)),
                  pl.BlockSpec((None, d, tf), lambda i, f: (layer, 0, f)),
                  pl.BlockSpec((None, tf, d), lambda i, f: (layer, f, 0)),
                  pl.BlockSpec((None, 1, d), lambda i, f: (layer, 0, 0)),
                  pl.BlockSpec((None, 1, d), lambda i, f: (layer, 0, 0))],
        out_specs=pl.BlockSpec((tm, d), lambda i, f: (i, 0)),
        out_shape=jax.ShapeDtypeStruct((m, d), F32),
        scratch_shapes=[pltpu.VMEM((tm, d), F32), pltpu.VMEM((tm, d), BF16)],
        compiler_params=_params("parallel", "arbitrary"),
        name="mlp",
    )(x, wu_stack, wd_stack, g, b)


def _sb_prompt_kernel(q_ref, k_ref, v_ref, uu_ref, g_ref, o_ref, acc_ref, carry_ref, qb_ref):
    qi = pl.program_id(2)
    t = q_ref.shape[0]
    chunks = [slice(c * SUFFIX_CHUNK, (c + 1) * SUFFIX_CHUNK) for c in range(t // SUFFIX_CHUNK)]
    acc_ref[...] = jnp.zeros_like(acc_ref)
    carry_ref[...] = jnp.zeros_like(carry_ref)
    qb_ref[...] = q_ref[...].astype(BF16)

    def tile(kj, diagonal):
        rows = pl.ds(pl.multiple_of(kj * t, t), t)
        z = _dot_nt(qb_ref[...], k_ref[rows, :].astype(BF16)) * ATTN_SCALE
        nsp = _neg_softplus(z)
        if diagonal:
            past = lax.broadcasted_iota(jnp.int32, (t, t), 1) < lax.broadcasted_iota(jnp.int32, (t, t), 0)
            lk = jnp.where(past, nsp, 0.0)
        else:
            lk = nsp
        hi, lo = _split(lk)
        uu = uu_ref[...]
        locs = [jnp.dot(jnp.concatenate([hi[:, c], lo[:, c]], axis=1), uu, preferred_element_type=F32)
                for c in chunks]
        run = carry_ref[...]
        between = [None] * len(chunks)
        for i in reversed(range(len(chunks))):
            between[i] = locs[i] + run
            run = run + (locs[i][:, 0:1] + lk[:, chunks[i].start:chunks[i].start + 1])
        w = jnp.exp(z + nsp + jnp.concatenate(between, axis=1))
        if diagonal:
            w = jnp.where(past, w, 0.0)
        acc_ref[...] += jnp.dot(w.astype(BF16), v_ref[rows, :].astype(BF16), preferred_element_type=F32)
        carry_ref[...] = run

    tile(qi, True)

    def older(i, carry):
        tile(qi - 1 - i, False)
        return carry

    lax.fori_loop(0, qi, older, 0)
    o_ref[...] = _head_rms(acc_ref[...], g_ref[...]).astype(o_ref.dtype)


def _moba_prompt_kernel(slope_ref, q_ref, k_ref, v_ref, km_ref, g_ref, o_ref,
                        acc_ref, m_ref, l_ref, qb_ref, gate_ref, bias_ref, *, n_top, blocks_per_step):
    h = pl.program_id(1)
    qi = pl.program_id(2)
    t = MOBA_BLOCK
    slope2 = slope_ref[h] * LOG2E
    own_tile = qi // blocks_per_step
    own_slot = qi % blocks_per_step

    acc_ref[...] = jnp.zeros_like(acc_ref)
    m_ref[...] = jnp.full_like(m_ref, NEG_INF)
    l_ref[...] = jnp.zeros_like(l_ref)
    qb_ref[...] = (q_ref[...] * (ATTN_SCALE * LOG2E)).astype(BF16)
    gate_ref[...] = lax.dot_general(q_ref[...], km_ref[...], (((1,), (1,)), ((), ())),
                                    preferred_element_type=F32, precision=lax.Precision.HIGHEST)
    bias_ref[...] = (lax.broadcasted_iota(jnp.int32, (t, t), 1)
                     - lax.broadcasted_iota(jnp.int32, (t, t), 0)).astype(F32) * slope2

    def selected(blk):
        gate = gate_ref[...]
        lane = lax.broadcasted_iota(jnp.int32, gate.shape, 1)
        g_j = jnp.sum(jnp.where(lane == blk, gate, 0.0), axis=-1, keepdims=True)
        beats = ((gate > g_j) | ((gate == g_j) & (lane < blk))) & (lane < qi)
        return jnp.sum(jnp.where(beats, 1.0, 0.0), axis=-1, keepdims=True) < n_top

    def update(kj, slots):
        rows = [pl.ds(pl.multiple_of(kj * (blocks_per_step * t) + s * t, t), t) for s, _, _ in slots]
        z = _dot_nt(qb_ref[...], jnp.concatenate([k_ref[r, :] for r in rows], axis=0).astype(BF16))
        bias = bias_ref[...]
        parts = []
        for i, (_, blk, kind) in enumerate(slots):
            s = (z[:, i * t:(i + 1) * t] - slope2 * ((qi - blk) * t).astype(F32)) + bias
            if kind == "own":
                mask = lax.broadcasted_iota(jnp.int32, (t, t), 1) <= lax.broadcasted_iota(jnp.int32, (t, t), 0)
            else:
                mask = selected(blk)
            parts.append(jnp.where(mask, s, NEG_INF))
        m_old = m_ref[...]
        m_new = m_old
        for s in parts:
            m_new = jnp.maximum(m_new, jnp.max(s, axis=-1, keepdims=True))
        a = jnp.exp2(m_old - m_new)
        es = [jnp.exp2(s - m_new) for s in parts]
        l_new = a * l_ref[...]
        for e in es:
            l_new = l_new + jnp.sum(e, axis=-1, keepdims=True)
        e_all = jnp.concatenate(es, axis=1).astype(BF16)
        v_all = jnp.concatenate([v_ref[r, :] for r in rows], axis=0).astype(BF16)
        acc_ref[...] = a * acc_ref[...] + jnp.dot(e_all, v_all, preferred_element_type=F32)
        l_ref[...] = l_new
        m_ref[...] = m_new

    for r in range(blocks_per_step):
        @pl.when(own_slot == r)
        def _(r=r):
            update(own_tile, [(r, qi, "own")] + [(i, qi - r + i, "past") for i in range(r)])

    def older(kj, carry):
        update(kj, [(i, blocks_per_step * kj + i, "past") for i in range(blocks_per_step)])
        return carry

    lax.fori_loop(0, own_tile, older, 0)
    o_ref[...] = _head_rms(acc_ref[...] / l_ref[...], g_ref[...]).astype(o_ref.dtype)


def _mem_prompt_kernel(q_ref, k_ref, v_ref, g_ref, o_ref):
    s = _dot_nt(q_ref[...].astype(BF16), k_ref[...].astype(BF16)) * ATTN_SCALE
    e = jnp.exp(s - jnp.max(s, axis=-1, keepdims=True))
    o = jnp.dot(e.astype(BF16), v_ref[...].astype(BF16), preferred_element_type=F32)
    o = o / jnp.sum(e, axis=-1, keepdims=True)
    o_ref[...] = _head_rms(o, g_ref[...]).astype(o_ref.dtype)


def _block_mean_kernel(k_ref, o_ref, *, group):
    x = k_ref[...]
    o_ref[...] = jnp.sum(x.reshape(group, MOBA_BLOCK, x.shape[-1]), axis=1) * (1.0 / MOBA_BLOCK)


def _prompt_attention(proj, memkv, gain, slopes, uu, layer, batch, seq, h_sb, h_mb, h_mem, mem_len):
    rows = batch * seq
    hd = HEAD_DIM
    c_qsb, c_ksb, c_vsb = 0, h_sb, 2 * h_sb
    c_qmb, c_kmb, c_vmb = 3 * h_sb, 3 * h_sb + h_mb, 3 * h_sb + 2 * h_mb
    c_qmem = 3 * h_sb + 3 * h_mb
    assert c_kmb % h_mb == 0

    t = min(SB_TILE, seq)
    n_t = seq // t
    o_sb = pl.pallas_call(
        _sb_prompt_kernel,
        grid=(batch, h_sb, n_t),
        in_specs=[pl.BlockSpec((t, hd), lambda b, h, i: (b * n_t + i, c_qsb + h)),
                  pl.BlockSpec((seq, hd), lambda b, h, i: (b, c_ksb + h)),
                  pl.BlockSpec((seq, hd), lambda b, h, i: (b, c_vsb + h)),
                  pl.BlockSpec((2 * SUFFIX_CHUNK, SUFFIX_CHUNK), lambda b, h, i: (0, 0)),
                  pl.BlockSpec((None, 1, hd), lambda b, h, i: (layer, 0, h))],
        out_specs=pl.BlockSpec((t, hd), lambda b, h, i: (b * n_t + i, h)),
        out_shape=jax.ShapeDtypeStruct((rows, h_sb * hd), BF16),
        scratch_shapes=[pltpu.VMEM((t, hd), F32), pltpu.VMEM((t, 1), F32), pltpu.VMEM((t, hd), BF16)],
        compiler_params=_params("parallel", "parallel", "arbitrary"),
        name="prompt_sb_attn",
    )(proj, proj, proj, uu, gain)

    nblk = seq // MOBA_BLOCK
    group = min(8, batch * nblk)
    kmean = pl.pallas_call(
        functools.partial(_block_mean_kernel, group=group),
        grid=(batch * nblk // group,),
        in_specs=[pl.BlockSpec((group * MOBA_BLOCK, h_mb * hd), lambda i: (i, c_kmb // h_mb))],
        out_specs=pl.BlockSpec((group, h_mb * hd), lambda i: (i, 0)),
        out_shape=jax.ShapeDtypeStruct((batch * nblk, h_mb * hd), F32),
        compiler_params=_params("parallel"),
        name="moba_block_mean",
    )(proj)
    kmean = kmean.reshape(batch, nblk, h_mb, hd).transpose(0, 2, 1, 3)
    kmean = jnp.pad(kmean, ((0, 0), (0, 0), (0, hd - nblk), (0, 0)))

    blocks_per_step = MOBA_KEYS // MOBA_BLOCK
    o_mb = pl.pallas_call(
        functools.partial(_moba_prompt_kernel, n_top=min(MOBA_TOPK, nblk), blocks_per_step=blocks_per_step),
        grid=(batch, h_mb, nblk),
        in_specs=[pl.BlockSpec(memory_space=pltpu.SMEM),
                  pl.BlockSpec((MOBA_BLOCK, hd), lambda b, h, i: (b * nblk + i, c_qmb + h)),
                  pl.BlockSpec((seq, hd), lambda b, h, i: (b, c_kmb + h)),
                  pl.BlockSpec((seq, hd), lambda b, h, i: (b, c_vmb + h)),
                  pl.BlockSpec((None, None, hd, hd), lambda b, h, i: (b, h, 0, 0)),
                  pl.BlockSpec((None, 1, hd), lambda b, h, i: (layer, 0, h_sb + h))],
        out_specs=pl.BlockSpec((MOBA_BLOCK, hd), lambda b, h, i: (b * nblk + i, h)),
        out_shape=jax.ShapeDtypeStruct((rows, h_mb * hd), BF16),
        scratch_shapes=[pltpu.VMEM((MOBA_BLOCK, hd), F32), pltpu.VMEM((MOBA_BLOCK, 1), F32),
                        pltpu.VMEM((MOBA_BLOCK, 1), F32), pltpu.VMEM((MOBA_BLOCK, hd), BF16),
                        pltpu.VMEM((MOBA_BLOCK, hd), F32), pltpu.VMEM((MOBA_BLOCK, MOBA_BLOCK), F32)],
        compiler_params=_params("parallel", "parallel", "arbitrary"),
        name="prompt_moba_attn",
    )(slopes, proj, proj, proj, kmean, gain)

    tq = min(512, seq)
    nq = seq // tq
    o_mem = pl.pallas_call(
        _mem_prompt_kernel,
        grid=(batch, h_mem, nq),
        in_specs=[pl.BlockSpec((tq, hd), lambda b, h, i: (b * nq + i, c_qmem + h)),
                  pl.BlockSpec((mem_len, hd), lambda b, h, i: (b, h)),
                  pl.BlockSpec((mem_len, hd), lambda b, h, i: (b, h_mem + h)),
                  pl.BlockSpec((None, 1, hd), lambda b, h, i: (layer, 0, h_sb + h_mb + h))],
        out_specs=pl.BlockSpec((tq, hd), lambda b, h, i: (b * nq + i, h)),
        out_shape=jax.ShapeDtypeStruct((rows, h_mem * hd), BF16),
        compiler_params=_params("parallel", "parallel", "parallel"),
        name="prompt_mem_attn",
    )(proj, memkv, memkv, gain)
    return jnp.concatenate([o_sb, o_mb, o_mem], axis=1)


def _kv_rows_kernel(*refs, n):
    for src_ref, dst_ref in zip(refs[:n], refs[2 * n:]):
        dst_ref[...] = src_ref[...]


def _store_kv_rows(proj, bufs, col_blocks, layer, batch, seq, n_heads):
    n = len(bufs)
    hd = HEAD_DIM
    tt = min(1024, seq)
    n_t = seq // tt
    return pl.pallas_call(
        functools.partial(_kv_rows_kernel, n=n),
        grid=(batch, n_heads, n_t),
        in_specs=[pl.BlockSpec((tt, hd), functools.partial(lambda b, h, i, c: (b * n_t + i, c + h), c=c))
                  for c in col_blocks] + [pl.BlockSpec(memory_space=pl.ANY)] * n,
        out_specs=[pl.BlockSpec((None, None, None, tt, hd), lambda b, h, i: (layer, b, h, i, 0))] * n,
        out_shape=[jax.ShapeDtypeStruct(buf.shape, buf.dtype) for buf in bufs],
        input_output_aliases={n + k: k for k in range(n)},
        compiler_params=_params("parallel", "parallel", "parallel"),
        name="store_kv_rows",
    )(*([proj] * n), *bufs)


def _dot3(x, w):
    xh, xl = _split(x)
    wh, wl = _split(w)
    m = x.shape[0]
    a = jnp.dot(jnp.concatenate([xh, xl], axis=0), wh, preferred_element_type=F32)
    return a[:m] + a[m:] + jnp.dot(xh, wl, preferred_element_type=F32)


def _dot3_nt(x, k):
    xh, xl = _split(x)
    kh, kl = _split(k)
    m = x.shape[0]
    a = _dot_nt(jnp.concatenate([xh, xl], axis=0), kh)
    return a[:m] + a[m:] + _dot_nt(xh, kl)


def _proj3_kernel(x_ref, w_ref, o_ref):
    o_ref[...] = _dot3(x_ref[...], w_ref[...])


def _proj3(x, w_stack, layer, tn):
    m, k = x.shape
    n = w_stack.shape[2]
    tn = min(tn, n)
    return pl.pallas_call(
        _proj3_kernel,
        grid=(n // tn,),
        in_specs=[pl.BlockSpec((m, k), lambda j: (0, 0)),
                  pl.BlockSpec((None, k, tn), lambda j: (layer, 0, j))],
        out_specs=pl.BlockSpec((m, tn), lambda j: (0, j)),
        out_shape=jax.ShapeDtypeStruct((m, n), F32),
        compiler_params=_params("parallel"),
        name="sample_proj",
    )(x, w_stack)


def _out_ln3_kernel(x_ref, w_ref, r_ref, g_ref, b_ref, o_ref, acc_ref, *, alpha):
    k = pl.program_id(0)

    @pl.when(k == 0)
    def _():
        acc_ref[...] = jnp.zeros_like(acc_ref)

    acc_ref[...] += _dot3(x_ref[...], w_ref[...])

    @pl.when(k == pl.num_programs(0) - 1)
    def _():
        o_ref[...] = _layer_norm(alpha * r_ref[...] + acc_ref[...], g_ref[...], b_ref[...])


def _out_ln3(x, w_stack, res, g, b, layer, alpha, tk):
    m, kdim = x.shape
    n = w_stack.shape[2]
    tk = min(tk, kdim)
    return pl.pallas_call(
        functools.partial(_out_ln3_kernel, alpha=alpha),
        grid=(kdim // tk,),
        in_specs=[pl.BlockSpec((m, tk), lambda k: (0, k)),
                  pl.BlockSpec((None, tk, n), lambda k: (layer, k, 0)),
                  pl.BlockSpec((m, n), lambda k: (0, 0)),
                  pl.BlockSpec((None, 1, n), lambda k: (layer, 0, 0)),
                  pl.BlockSpec((None, 1, n), lambda k: (layer, 0, 0))],
        out_specs=pl.BlockSpec((m, n), lambda k: (0, 0)),
        out_shape=jax.ShapeDtypeStruct((m, n), F32),
        scratch_shapes=[pltpu.VMEM((m, n), F32)],
        compiler_params=_params("arbitrary"),
        name="sample_out_ln",
    )(x, w_stack, res, g, b)


def _mlp3_kernel(x_ref, wu_ref, wd_ref, g_ref, b_ref, o_ref, acc_ref, *, alpha):
    f = pl.program_id(0)

    @pl.when(f == 0)
    def _():
        acc_ref[...] = jnp.zeros_like(acc_ref)

    h = jnp.maximum(_dot3(x_ref[...], wu_ref[...]), 0.0)
    acc_ref[...] += _dot3(h * h, wd_ref[...])

    @pl.when(f == pl.num_programs(0) - 1)
    def _():
        o_ref[...] = _layer_norm(alpha * x_ref[...] + acc_ref[...], g_ref[...], b_ref[...])


def _mlp3(x, wu_stack, wd_stack, g, b, layer, alpha, tf):
    m, d = x.shape
    ff = wu_stack.shape[2]
    tf = min(tf, ff)
    return pl.pallas_call(
        functools.partial(_mlp3_kernel, alpha=alpha),
        grid=(ff // tf,),
        in_specs=[pl.BlockSpec((m, d), lambda f: (0, 0)),
                  pl.BlockSpec((None, d, tf), lambda f: (layer, 0, f)),
                  pl.BlockSpec((None, tf, d), lambda f: (layer, f, 0)),
                  pl.BlockSpec((None, 1, d), lambda f: (layer, 0, 0)),
                  pl.BlockSpec((None, 1, d), lambda f: (layer, 0, 0))],
        out_specs=pl.BlockSpec((m, d), lambda f: (0, 0)),
        out_shape=jax.ShapeDtypeStruct((m, d), F32),
        scratch_shapes=[pltpu.VMEM((m, d), F32)],
        compiler_params=_params("arbitrary"),
        name="sample_mlp",
    )(x, wu_stack, wd_stack, g, b)


def _pick_rows(parts):
    row = lax.broadcasted_iota(jnp.int32, parts[0].shape, 0)
    out = jnp.zeros_like(parts[0])
    for h, part in enumerate(parts):
        out = jnp.where(row == h, part, out)
    return out


def _own_head_blocks(x, n_heads):
    return _pick_rows([x[:, h * HEAD_DIM:(h + 1) * HEAD_DIM] for h in range(n_heads)])


def _head_scores(q, keys):
    row = lax.broadcasted_iota(jnp.int32, q.shape, 0)
    q_bd = jnp.concatenate([jnp.where(row == h, q, 0.0) for h in range(len(keys))], axis=1)
    return _dot3_nt(q_bd, jnp.concatenate(keys, axis=1)) * ATTN_SCALE


def _head_values(w, values):
    return _dot3(w, jnp.concatenate(values, axis=1))


def _suffix_sums3(lk, uu):
    hi, lo = _split(lk)
    lo2 = (lk - hi.astype(F32) - lo.astype(F32)).astype(BF16)
    loc = jnp.dot(jnp.concatenate([hi, lo], axis=1), uu, preferred_element_type=F32)
    loc = loc + jnp.dot(lo2, uu[:SUFFIX_CHUNK], preferred_element_type=F32)
    return loc, loc[:, 0:1] + lk[:, 0:1]


def _sb_decode_kernel(pt_ref, q_ref, *refs, n_heads, n_sub):
    k_refs, v_refs = refs[:n_sub], refs[n_sub:2 * n_sub]
    uu_ref, g_ref, o_ref, acc_ref, carry_ref = refs[2 * n_sub:]
    j = pl.program_id(1)

    @pl.when(j == 0)
    def _():
        acc_ref[...] = jnp.zeros_like(acc_ref)
        carry_ref[...] = jnp.zeros_like(carry_ref)

    def tokens(page_refs, h):
        return jnp.concatenate([r[h] for r in page_refs], axis=0)

    z = _head_scores(q_ref[...], [tokens(k_refs, h) for h in range(n_heads)])
    nsp = _neg_softplus(z)
    uu = uu_ref[...]
    run = carry_ref[...]
    n_chunks = z.shape[1] // SUFFIX_CHUNK
    between = [None] * n_chunks
    for c in reversed(range(n_chunks)):
        loc, tot = _suffix_sums3(nsp[:, c * SUFFIX_CHUNK:(c + 1) * SUFFIX_CHUNK], uu)
        between[c] = loc + run
        run = run + tot
    w = jnp.exp(z + nsp + jnp.concatenate(between, axis=1))
    acc_ref[...] += _head_values(w, [tokens(v_refs, h) for h in range(n_heads)])
    carry_ref[...] = run

    @pl.when(j == pl.num_programs(1) - 1)
    def _():
        o_ref[...] = _head_rms(_own_head_blocks(acc_ref[...], n_heads), g_ref[...])


def _moba_gate_kernel(pt_ref, q_ref, *refs, n_top, n_sub, pages_per_block):
    k_refs = refs[:n_sub * pages_per_block]
    idx_ref, gate_ref = refs[n_sub * pages_per_block:]
    n = pl.program_id(1)

    @pl.when(n == 0)
    def _():
        gate_ref[...] = jnp.full_like(gate_ref, NEG_INF)

    n_heads = k_refs[0].shape[0]
    lane = lax.broadcasted_iota(jnp.int32, (n_heads, HEAD_DIM), 1)
    gate = gate_ref[0:n_heads, :]
    for b in range(n_sub):
        pages = k_refs[b * pages_per_block:(b + 1) * pages_per_block]
        mean = sum(jnp.sum(r[...], axis=1) for r in pages) * (1.0 / MOBA_BLOCK)
        g = jnp.sum(q_ref[0:n_heads, :] * mean, axis=-1, keepdims=True)
        gate = jnp.where(lane == n * n_sub + b, g, gate)
    gate_ref[0:n_heads, :] = gate

    @pl.when(n == pl.num_programs(1) - 1)
    def _():
        gate = gate_ref[...]
        lanes = lax.broadcasted_iota(jnp.int32, gate.shape, 1)
        idx = jnp.zeros(gate.shape, jnp.int32)
        for r in range(n_top):
            best = jnp.max(gate, axis=-1, keepdims=True)
            first = jnp.min(jnp.where(gate == best, lanes, HEAD_DIM), axis=-1, keepdims=True)
            idx = jnp.where(lanes == r, first, idx)
            gate = jnp.where(lanes == first, NEG_INF, gate)
        idx_ref[...] = idx


def _moba_decode_kernel(pt_ref, top_ref, slope_ref, q_ref, kn_ref, vn_ref, g_ref, *refs, n_heads, past_len, page):
    k_refs, v_refs = refs[:n_heads], refs[n_heads:2 * n_heads]
    o_ref, acc_ref, m_ref, l_ref = refs[2 * n_heads:]
    s_id, t = pl.program_id(0), pl.program_id(1)
    pages_per_block = MOBA_BLOCK // page
    q = q_ref[...]

    @pl.when(t == 0)
    def _():
        m_ref[...] = jnp.sum(q * kn_ref[...], axis=-1, keepdims=True) * ATTN_SCALE
        l_ref[...] = jnp.ones_like(l_ref)
        acc_ref[...] = vn_ref[...]

    row = lax.broadcasted_iota(jnp.int32, (8, 1), 0)
    slope = jnp.zeros((8, 1), F32)
    dist0 = jnp.zeros((8, 1), F32)
    for h in range(n_heads):
        kpos0 = top_ref[s_id, h, t // pages_per_block] * MOBA_BLOCK + (t % pages_per_block) * page
        slope = jnp.where(row == h, slope_ref[h], slope)
        dist0 = jnp.where(row == h, (past_len - kpos0).astype(F32), dist0)
    lane = lax.broadcasted_iota(jnp.int32, (8, page), 1)
    s = _head_scores(q, [r[...] for r in k_refs]) - slope * (dist0 - lane.astype(F32))
    m_old = m_ref[...]
    m_new = jnp.maximum(m_old, jnp.max(s, axis=-1, keepdims=True))
    a = jnp.exp(m_old - m_new)
    e = jnp.exp(s - m_new)
    l_ref[...] = a * l_ref[...] + jnp.sum(e, axis=-1, keepdims=True)
    acc_ref[...] = a * acc_ref[...] + _own_head_blocks(_head_values(e, [r[...] for r in v_refs]), n_heads)
    m_ref[...] = m_new

    @pl.when(t == pl.num_programs(1) - 1)
    def _():
        o_ref[...] = _head_rms(acc_ref[...] / l_ref[...], g_ref[...])


def _mem_decode_kernel(q_ref, k_ref, v_ref, g_ref, o_ref, *, n_heads):
    s = _head_scores(q_ref[...], [k_ref[:, h, :] for h in range(n_heads)])
    e = jnp.exp(s - jnp.max(s, axis=-1, keepdims=True))
    o = _head_values(e, [v_ref[:, h, :] for h in range(n_heads)])
    o = _own_head_blocks(o, n_heads) / jnp.sum(e, axis=-1, keepdims=True)
    o_ref[...] = _head_rms(o, g_ref[...])


def _heads_to_rows(x, n_heads):
    x = x.reshape(x.shape[0], n_heads, HEAD_DIM)
    return jnp.pad(x, ((0, 0), (0, 8 - n_heads), (0, 0)))


def _pages_per_step(n, most):
    return max(d for d in range(1, most + 1) if n % d == 0)


def _sample_attention(proj, gain_l, page_table, pools, mem_k, mem_v, slopes, uu, layer, h_sb, h_mb, h_mem):
    cache_k_sb, cache_v_sb, cache_k_mb, cache_v_mb = pools
    n_seq, n_pages = page_table.shape
    page = cache_k_sb.shape[3]
    past_len = n_pages * page
    hd = HEAD_DIM
    w_sb, w_mb = h_sb * hd, h_mb * hd
    q_sb = _heads_to_rows(proj[:, 0:w_sb], h_sb)
    q_mb = _heads_to_rows(proj[:, 3 * w_sb:3 * w_sb + w_mb], h_mb)
    k_mb_new = _heads_to_rows(proj[:, 3 * w_sb + w_mb:3 * w_sb + 2 * w_mb], h_mb)
    v_mb_new = _heads_to_rows(proj[:, 3 * w_sb + 2 * w_mb:3 * w_sb + 3 * w_mb], h_mb)
    q_mem = _heads_to_rows(proj[:, 3 * w_sb + 3 * w_mb:], h_mem)
    g_sb = _heads_to_rows(gain_l[None, 0:w_sb], h_sb)[0]
    g_mb = _heads_to_rows(gain_l[None, w_sb:w_sb + w_mb], h_mb)[0]
    g_mem = _heads_to_rows(gain_l[None, w_sb + w_mb:], h_mem)[0]

    row_spec = pl.BlockSpec((None, 8, hd), lambda s, j, *_: (s, 0, 0))
    gain_spec = pl.BlockSpec((8, hd), lambda s, j, *_: (0, 0))
    row_shape = jax.ShapeDtypeStruct((n_seq, 8, hd), F32)

    def page_spec(n_heads, page_of):
        return pl.BlockSpec((None, None, n_heads, page, hd),
                            lambda s, j, pt: (layer, pt[s, page_of(j)], 0, 0, 0))

    n_sub = _pages_per_step(n_pages, 8)
    sb_pages = [page_spec(h_sb, functools.partial(lambda j, i: n_pages - n_sub * (j + 1) + i, i=i))
                for i in range(n_sub)]
    o_sb = pl.pallas_call(
        functools.partial(_sb_decode_kernel, n_heads=h_sb, n_sub=n_sub),
        grid_spec=pltpu.PrefetchScalarGridSpec(
            num_scalar_prefetch=1, grid=(n_seq, n_pages // n_sub),
            in_specs=[row_spec] + sb_pages + sb_pages
                     + [pl.BlockSpec((2 * SUFFIX_CHUNK, SUFFIX_CHUNK), lambda s, j, pt: (0, 0)), gain_spec],
            out_specs=row_spec,
            scratch_shapes=[pltpu.VMEM((8, h_sb * hd), F32), pltpu.VMEM((8, 1), F32)]),
        out_shape=row_shape,
        compiler_params=_params("parallel", "arbitrary"),
        name="sample_sb_attn",
    )(page_table, q_sb, *([cache_k_sb] * n_sub), *([cache_v_sb] * n_sub), uu, g_sb)

    pages_per_block = MOBA_BLOCK // page
    n_past_blocks = past_len // MOBA_BLOCK
    n_top = min(MOBA_TOPK, n_past_blocks)
    n_gate = _pages_per_step(n_past_blocks, 8)
    gate_pages = [page_spec(h_mb, functools.partial(lambda n, i: n_gate * pages_per_block * n + i, i=i))
                  for i in range(n_gate * pages_per_block)]
    top, _ = pl.pallas_call(
        functools.partial(_moba_gate_kernel, n_top=n_top, n_sub=n_gate, pages_per_block=pages_per_block),
        grid_spec=pltpu.PrefetchScalarGridSpec(
            num_scalar_prefetch=1, grid=(n_seq, n_past_blocks // n_gate),
            in_specs=[row_spec] + gate_pages,
            out_specs=[row_spec, row_spec]),
        out_shape=[jax.ShapeDtypeStruct((n_seq, 8, hd), jnp.int32), row_shape],
        compiler_params=_params("parallel", "arbitrary"),
        name="sample_moba_gate",
    )(page_table, q_mb, *([cache_k_mb] * (n_gate * pages_per_block)))
    top = top[:, :, :max(n_top, 1)]

    def sel_page(h):
        return pl.BlockSpec(
            (None, None, None, page, hd),
            lambda s, t, pt, tp: (layer, pt[s, tp[s, h, t // pages_per_block] * pages_per_block + t % pages_per_block],
                                  h, 0, 0))

    sel_pages = [sel_page(h) for h in range(h_mb)]
    o_mb = pl.pallas_call(
        functools.partial(_moba_decode_kernel, n_heads=h_mb, past_len=past_len, page=page),
        grid_spec=pltpu.PrefetchScalarGridSpec(
            num_scalar_prefetch=2, grid=(n_seq, n_top * pages_per_block),
            in_specs=[pl.BlockSpec(memory_space=pltpu.SMEM), row_spec, row_spec, row_spec, gain_spec]
                     + sel_pages + sel_pages,
            out_specs=row_spec,
            scratch_shapes=[pltpu.VMEM((8, hd), F32), pltpu.VMEM((8, 1), F32), pltpu.VMEM((8, 1), F32)]),
        out_shape=row_shape,
        compiler_params=_params("parallel", "arbitrary"),
        name="sample_moba_attn",
    )(page_table, top, slopes, q_mb, k_mb_new, v_mb_new, g_mb, *([cache_k_mb] * h_mb), *([cache_v_mb] * h_mb))

    mem_len = mem_k.shape[2]
    mem_spec = pl.BlockSpec((None, None, mem_len, h_mem, hd), lambda s: (layer, s, 0, 0, 0))
    o_mem = pl.pallas_call(
        functools.partial(_mem_decode_kernel, n_heads=h_mem),
        grid=(n_seq,),
        in_specs=[pl.BlockSpec((None, 8, hd), lambda s: (s, 0, 0)), mem_spec, mem_spec,
                  pl.BlockSpec((8, hd), lambda s: (0, 0))],
        out_specs=pl.BlockSpec((None, 8, hd), lambda s: (s, 0, 0)),
        out_shape=row_shape,
        compiler_params=_params("parallel"),
        name="sample_mem_attn",
    )(q_mem, mem_k, mem_v, g_mem)

    return jnp.concatenate([o_sb[:, :h_sb].reshape(n_seq, w_sb), o_mb[:, :h_mb].reshape(n_seq, w_mb),
                            o_mem[:, :h_mem].reshape(n_seq, h_mem * hd)], axis=1)


def kernel(x_prompt, x_sample, mem_prompt, cache_k_sb, cache_v_sb, cache_k_moba, cache_v_moba, cache_mem_k, cache_mem_v, page_table, w_in, w_mem_kv, out_norm_g, w_o, ln1_g, ln1_b, w_up, w_down, ln2_g, ln2_b):
    batch, seq, d_model = x_prompt.shape
    n_seq, dec_seq, _ = x_sample.shape
    depth = w_in.shape[0]
    h_sb, h_mb, h_mem = cache_k_sb.shape[3], cache_k_moba.shape[3], cache_mem_k.shape[3]
    mem_len = mem_prompt.shape[1]
    hd = HEAD_DIM
    assert dec_seq == 1 and cache_k_sb.shape[4] == hd and seq % MOBA_KEYS == 0
    assert MOBA_BLOCK % cache_k_sb.shape[2] == 0 and seq // MOBA_BLOCK <= hd
    w_mem = h_mem * hd
    alpha = (2 * depth) ** 0.25

    w_in_b, w_kv_b, w_o_b = w_in.astype(BF16), w_mem_kv.astype(BF16), w_o.astype(BF16)
    w_up_b, w_down_b = w_up.astype(BF16), w_down.astype(BF16)
    gain = out_norm_g.reshape(depth, 1, d_model)
    ln1_g3, ln1_b3 = ln1_g.reshape(depth, 1, d_model), ln1_b.reshape(depth, 1, d_model)
    ln2_g3, ln2_b3 = ln2_g.reshape(depth, 1, d_model), ln2_b.reshape(depth, 1, d_model)
    slopes = jnp.exp2(-8.0 * jnp.arange(1, h_mb + 1, dtype=F32) / h_mb)
    uu = _suffix_matrix()
    pools = tuple(jnp.transpose(c, (0, 1, 3, 2, 4)) for c in (cache_k_sb, cache_v_sb, cache_k_moba, cache_v_moba))

    hp = x_prompt.reshape(batch * seq, d_model)
    hs = x_sample.reshape(n_seq, d_model)
    mem2d = mem_prompt.reshape(batch * mem_len, d_model)
    assert h_sb == h_mb
    kv_p = [jnp.zeros((depth, batch, h_sb, seq, hd), F32) for _ in range(4)]
    kv_cols = (h_sb, 2 * h_sb, 3 * h_sb + h_mb, 3 * h_sb + 2 * h_mb)
    outs = [[] for _ in range(10)]
    for l in range(depth):
        memkv = _proj(mem2d, w_kv_b, l, 512, 512)
        proj_p = _proj(hp, w_in_b, l, 1024, 1024)
        mix_p = _prompt_attention(proj_p, memkv, gain, slopes, uu, l, batch, seq, h_sb, h_mb, h_mem, mem_len)
        hp = _out_ln(mix_p, w_o_b, hp, ln1_g3, ln1_b3, l, alpha, 1024, 512)
        hp = _mlp(hp, w_up_b, w_down_b, ln2_g3, ln2_b3, l, alpha, 512, 512)

        proj_s = _proj3(hs, w_in, l, 512)
        mix_s = _sample_attention(proj_s, out_norm_g[l], page_table, pools, cache_mem_k, cache_mem_v,
                                  slopes, uu, l, h_sb, h_mb, h_mem)
        hs = _out_ln3(mix_s, w_o, hs, ln1_g3, ln1_b3, l, alpha, 512)
        hs = _mlp3(hs, w_up, w_down, ln2_g3, ln2_b3, l, alpha, 512)

        kv_p = _store_kv_rows(proj_p, kv_p, kv_cols, l, batch, seq, h_sb)
        for i, c in enumerate(kv_cols):
            outs[6 + i].append(proj_s[:, c * hd:(c + h_sb) * hd].reshape(n_seq, 1, h_sb, hd))
        outs[4].append(memkv[:, :w_mem].reshape(batch, mem_len, h_mem, hd))
        outs[5].append(memkv[:, w_mem:].reshape(batch, mem_len, h_mem, hd))
    return (hp.reshape(batch, seq, d_model), hs.reshape(n_seq, 1, d_model),
            *[jnp.transpose(buf, (0, 1, 3, 2, 4)) for buf in kv_p],
            *[jnp.stack(o) for o in outs[4:]])
```

```python
import functools

import numpy as np
import jax
import jax.numpy as jnp
from jax import lax
from jax.experimental import pallas as pl
from jax.experimental.pallas import tpu as pltpu

F32 = jnp.float32
BF16 = jnp.bfloat16

HEAD_DIM = 128
MOBA_BLOCK = 256
MOBA_TOPK = 3
LN_EPS = 1e-5
RMS_EPS = 1e-6
NEG_INF = -1e30
ATTN_SCALE = HEAD_DIM ** -0.5
SB_TILE = 512
MOBA_KEYS = 4 * MOBA_BLOCK
LOG2E = 1.4426950408889634
SUFFIX_CHUNK = 128
VMEM_LIMIT = 56 * 1024 * 1024


def _params(*sem):
    return pltpu.CompilerParams(dimension_semantics=sem, vmem_limit_bytes=VMEM_LIMIT)


def _dot_nt(a, b):
    return lax.dot_general(a, b, (((1,), (1,)), ((), ())), preferred_element_type=F32)


def _head_rms(o, g):
    return o * lax.rsqrt(jnp.mean(o * o, axis=-1, keepdims=True) + RMS_EPS) * g


def _layer_norm(y, g, b):
    mu = jnp.mean(y, axis=-1, keepdims=True)
    d = y - mu
    var = jnp.mean(d * d, axis=-1, keepdims=True)
    return d * lax.rsqrt(var + LN_EPS) * g + b


def _neg_softplus(z):
    return -(jnp.maximum(z, 0.0) + jnp.log(1.0 + jnp.exp(-jnp.abs(z))))


def _split(x):
    hi = x.astype(BF16)
    return hi, (x - hi.astype(F32)).astype(BF16)


def _suffix_matrix():
    j = np.arange(SUFFIX_CHUNK)
    u = (j[:, None] > j[None, :]).astype(np.float32)
    return jnp.asarray(np.concatenate([u, u], axis=0), dtype=BF16)


def _proj_kernel(x_ref, w_ref, o_ref, xb_ref):
    @pl.when(pl.program_id(1) == 0)
    def _():
        xb_ref[...] = x_ref[...].astype(BF16)

    o_ref[...] = jnp.dot(xb_ref[...], w_ref[...], preferred_element_type=F32)


def _proj(x, w_stack, layer, tm, tn):
    m, k = x.shape
    n = w_stack.shape[2]
    tm, tn = min(tm, m), min(tn, n)
    return pl.pallas_call(
        _proj_kernel,
        grid=(m // tm, n // tn),
        in_specs=[pl.BlockSpec((tm, k), lambda i, j: (i, 0)),
                  pl.BlockSpec((None, k, tn), lambda i, j: (layer, 0, j))],
        out_specs=pl.BlockSpec((tm, tn), lambda i, j: (i, j)),
        out_shape=jax.ShapeDtypeStruct((m, n), F32),
        scratch_shapes=[pltpu.VMEM((tm, k), BF16)],
        compiler_params=_params("parallel", "arbitrary"),
        name="proj",
    )(x, w_stack)


def _out_ln_kernel(x_ref, w_ref, r_ref, g_ref, b_ref, o_ref, acc_ref, *, alpha):
    k = pl.program_id(1)

    @pl.when(k == 0)
    def _():
        acc_ref[...] = jnp.zeros_like(acc_ref)

    acc_ref[...] += jnp.dot(x_ref[...], w_ref[...], preferred_element_type=F32)

    @pl.when(k == pl.num_programs(1) - 1)
    def _():
        o_ref[...] = _layer_norm(alpha * r_ref[...] + acc_ref[...], g_ref[...], b_ref[...])


def _out_ln(x, w_stack, res, g, b, layer, alpha, tm, tk):
    m, kdim = x.shape
    n = w_stack.shape[2]
    tm, tk = min(tm, m), min(tk, kdim)
    return pl.pallas_call(
        functools.partial(_out_ln_kernel, alpha=alpha),
        grid=(m // tm, kdim // tk),
        in_specs=[pl.BlockSpec((tm, tk), lambda i, k: (i, k)),
                  pl.BlockSpec((None, tk, n), lambda i, k: (layer, k, 0)),
                  pl.BlockSpec((tm, n), lambda i, k: (i, 0)),
                  pl.BlockSpec((None, 1, n), lambda i, k: (layer, 0, 0)),
                  pl.BlockSpec((None, 1, n), lambda i, k: (layer, 0, 0))],
        out_specs=pl.BlockSpec((tm, n), lambda i, k: (i, 0)),
        out_shape=jax.ShapeDtypeStruct((m, n), F32),
        scratch_shapes=[pltpu.VMEM((tm, n), F32)],
        compiler_params=_params("parallel", "arbitrary"),
        name="out_ln",
    )(x, w_stack, res, g, b)


def _mlp_kernel(x_ref, wu_ref, wd_ref, g_ref, b_ref, o_ref, acc_ref, xb_ref, *, alpha):
    f = pl.program_id(1)

    @pl.when(f == 0)
    def _():
        xb_ref[...] = x_ref[...].astype(BF16)
        acc_ref[...] = jnp.zeros_like(acc_ref)

    h = jnp.maximum(jnp.dot(xb_ref[...], wu_ref[...], preferred_element_type=F32), 0.0)
    acc_ref[...] += jnp.dot((h * h).astype(BF16), wd_ref[...], preferred_element_type=F32)

    @pl.when(f == pl.num_programs(1) - 1)
    def _():
        o_ref[...] = _layer_norm(alpha * x_ref[...] + acc_ref[...], g_ref[...], b_ref[...])


def _mlp(x, wu_stack, wd_stack, g, b, layer, alpha, tm, tf):
    m, d = x.shape
    ff = wu_stack.shape[2]
    tm, tf = min(tm, m), min(tf, ff)
    return pl.pallas_call(
        functools.partial(_mlp_kernel, alpha=alpha),
        grid=(m // tm, ff // tf),
        in_specs=[pl.BlockSpec((tm, d), lambda i, f: (i, 0)),
                  pl.BlockSpec((None, d, tf), lambda i, f: (layer, 0, f)),
                  pl.BlockSpec((None, tf, d), lambda i, f: (layer, f, 0)),
                  pl.BlockSpec((None, 1, d), lambda i, f: (layer, 0, 0)),
                  pl.BlockSpec((None, 1, d), lambda i, f: (layer, 0, 0))],
        out_specs=pl.BlockSpec((tm, d), lambda i, f: (i, 0)),
        out_shape=jax.ShapeDtypeStruct((m, d), F32),
        scratch_shapes=[pltpu.VMEM((tm, d), F32), pltpu.VMEM((tm, d), BF16)],
        compiler_params=_params("parallel", "arbitrary"),
        name="mlp",
    )(x, wu_stack, wd_stack, g, b)


def _sb_prompt_kernel(q_ref, k_ref, v_ref, uu_ref, g_ref, o_ref, acc_ref, carry_ref, qb_ref):
    qi = pl.program_id(2)
    t = q_ref.shape[0]
    chunks = [slice(c * SUFFIX_CHUNK, (c + 1) * SUFFIX_CHUNK) for c in range(t // SUFFIX_CHUNK)]
    acc_ref[...] = jnp.zeros_like(acc_ref)
    carry_ref[...] = jnp.zeros_like(carry_ref)
    qb_ref[...] = q_ref[...].astype(BF16)

    def tile(kj, diagonal):
        rows = pl.ds(pl.multiple_of(kj * t, t), t)
        z = _dot_nt(qb_ref[...], k_ref[rows, :].astype(BF16)) * ATTN_SCALE
        nsp = _neg_softplus(z)
        if diagonal:
            past = lax.broadcasted_iota(jnp.int32, (t, t), 1) < lax.broadcasted_iota(jnp.int32, (t, t), 0)
            lk = jnp.where(past, nsp, 0.0)
        else:
            lk = nsp
        hi, lo = _split(lk)
        uu = uu_ref[...]
        locs = [jnp.dot(jnp.concatenate([hi[:, c], lo[:, c]], axis=1), uu, preferred_element_type=F32)
                for c in chunks]
        run = carry_ref[...]
        between = [None] * len(chunks)
        for i in reversed(range(len(chunks))):
            between[i] = locs[i] + run
            run = run + (locs[i][:, 0:1] + lk[:, chunks[i].start:chunks[i].start + 1])
        w = jnp.exp(z + nsp + jnp.concatenate(between, axis=1))
        if diagonal:
            w = jnp.where(past, w, 0.0)
        acc_ref[...] += jnp.dot(w.astype(BF16), v_ref[rows, :].astype(BF16), preferred_element_type=F32)
        carry_ref[...] = run

    tile(qi, True)

    def older(i, carry):
        tile(qi - 1 - i, False)
        return carry

    lax.fori_loop(0, qi, older, 0)
    o_ref[...] = _head_rms(acc_ref[...], g_ref[...]).astype(o_ref.dtype)


def _moba_prompt_kernel(slope_ref, q_ref, k_ref, v_ref, km_ref, g_ref, o_ref,
                        acc_ref, m_ref, l_ref, qb_ref, sel_ref, bias_ref, *, n_top, n_blk, blocks_per_step):
    h = pl.program_id(1)
    qi = pl.program_id(2)
    t = MOBA_BLOCK
    slope2 = slope_ref[h] * LOG2E
    own_tile = qi // blocks_per_step
    own_slot = qi % blocks_per_step

    acc_ref[...] = jnp.zeros_like(acc_ref)
    m_ref[...] = jnp.full_like(m_ref, NEG_INF)
    l_ref[...] = jnp.zeros_like(l_ref)
    qb_ref[...] = (q_ref[...] * (ATTN_SCALE * LOG2E)).astype(BF16)
    gate = lax.dot_general(km_ref[...], q_ref[...], (((1,), (1,)), ((), ())),
                           preferred_element_type=F32, precision=lax.Precision.HIGHEST)
    blk_row = lax.broadcasted_iota(jnp.int32, (HEAD_DIM, t), 0)
    sel_t = jnp.zeros((HEAD_DIM, t), F32)
    for n in range(n_blk):
        g_n = gate[n:n + 1, :]
        beats = ((gate > g_n) | ((gate == g_n) & (blk_row < n))) & (blk_row < qi)
        rank = jnp.sum(jnp.where(beats, 1.0, 0.0), axis=0, keepdims=True)
        sel_t = jnp.where(blk_row == n, jnp.where(rank < n_top, 1.0, 0.0), sel_t)
    sel_ref[...] = sel_t.T
    bias_ref[...] = (lax.broadcasted_iota(jnp.int32, (t, t), 1)
                     - lax.broadcasted_iota(jnp.int32, (t, t), 0)).astype(F32) * slope2

    def selected(blk):
        sel = sel_ref[...]
        lane = lax.broadcasted_iota(jnp.int32, sel.shape, 1)
        return jnp.sum(jnp.where(lane == blk, sel, 0.0), axis=-1, keepdims=True) > 0.5

    def update(kj, slots):
        rows = [pl.ds(pl.multiple_of(kj * (blocks_per_step * t) + s * t, t), t) for s, _, _ in slots]
        z = _dot_nt(qb_ref[...], jnp.concatenate([k_ref[r, :] for r in rows], axis=0).astype(BF16))
        bias = bias_ref[...]
        parts = []
        for i, (_, blk, kind) in enumerate(slots):
            s = (z[:, i * t:(i + 1) * t] - slope2 * ((qi - blk) * t).astype(F32)) + bias
            if kind == "own":
                mask = lax.broadcasted_iota(jnp.int32, (t, t), 1) <= lax.broadcasted_iota(jnp.int32, (t, t), 0)
            else:
                mask = selected(blk)
            parts.append(jnp.where(mask, s, NEG_INF))
        m_old = m_ref[...]
        m_new = m_old
        for s in parts:
            m_new = jnp.maximum(m_new, jnp.max(s, axis=-1, keepdims=True))
        a = jnp.exp2(m_old - m_new)
        es = [jnp.exp2(s - m_new) for s in parts]
        l_new = a * l_ref[...]
        for e in es:
            l_new = l_new + jnp.sum(e, axis=-1, keepdims=True)
        e_all = jnp.concatenate(es, axis=1).astype(BF16)
        v_all = jnp.concatenate([v_ref[r, :] for r in rows], axis=0).astype(BF16)
        acc_ref[...] = a * acc_ref[...] + jnp.dot(e_all, v_all, preferred_element_type=F32)
        l_ref[...] = l_new
        m_ref[...] = m_new

    for r in range(blocks_per_step):
        @pl.when(own_slot == r)
        def _(r=r):
            update(own_tile, [(r, qi, "own")] + [(i, qi - r + i, "past") for i in range(r)])

    def older(kj, carry):
        update(kj, [(i, blocks_per_step * kj + i, "past") for i in range(blocks_per_step)])
        return carry

    lax.fori_loop(0, own_tile, older, 0)
    o_ref[...] = _head_rms(acc_ref[...] / l_ref[...], g_ref[...]).astype(o_ref.dtype)


def _mem_prompt_kernel(q_ref, k_ref, v_ref, g_ref, o_ref):
    s = _dot_nt(q_ref[...].astype(BF16), k_ref[...].astype(BF16)) * ATTN_SCALE
    e = jnp.exp(s - jnp.max(s, axis=-1, keepdims=True))
    o = jnp.dot(e.astype(BF16), v_ref[...].astype(BF16), preferred_element_type=F32)
    o = o / jnp.sum(e, axis=-1, keepdims=True)
    o_ref[...] = _head_rms(o, g_ref[...]).astype(o_ref.dtype)


def _block_mean_kernel(k_ref, o_ref, *, group):
    x = k_ref[...]
    o_ref[...] = jnp.sum(x.reshape(group, MOBA_BLOCK, x.shape[-1]), axis=1) * (1.0 / MOBA_BLOCK)


def _prompt_attention(proj, memkv, gain, slopes, uu, layer, batch, seq, h_sb, h_mb, h_mem, mem_len):
    rows = batch * seq
    hd = HEAD_DIM
    c_qsb, c_ksb, c_vsb = 0, h_sb, 2 * h_sb
    c_qmb, c_kmb, c_vmb = 3 * h_sb, 3 * h_sb + h_mb, 3 * h_sb + 2 * h_mb
    c_qmem = 3 * h_sb + 3 * h_mb
    assert c_kmb % h_mb == 0

    t = min(SB_TILE, seq)
    n_t = seq // t
    o_sb = pl.pallas_call(
        _sb_prompt_kernel,
        grid=(batch, h_sb, n_t),
        in_specs=[pl.BlockSpec((t, hd), lambda b, h, i: (b * n_t + i, c_qsb + h)),
                  pl.BlockSpec((seq, hd), lambda b, h, i: (b, c_ksb + h)),
                  pl.BlockSpec((seq, hd), lambda b, h, i: (b, c_vsb + h)),
                  pl.BlockSpec((2 * SUFFIX_CHUNK, SUFFIX_CHUNK), lambda b, h, i: (0, 0)),
                  pl.BlockSpec((None, 1, hd), lambda b, h, i: (layer, 0, h))],
        out_specs=pl.BlockSpec((t, hd), lambda b, h, i: (b * n_t + i, h)),
        out_shape=jax.ShapeDtypeStruct((rows, h_sb * hd), BF16),
        scratch_shapes=[pltpu.VMEM((t, hd), F32), pltpu.VMEM((t, 1), F32), pltpu.VMEM((t, hd), BF16)],
        compiler_params=_params("parallel", "parallel", "arbitrary"),
        name="prompt_sb_attn",
    )(proj, proj, proj, uu, gain)

    nblk = seq // MOBA_BLOCK
    group = min(8, batch * nblk)
    kmean = pl.pallas_call(
        functools.partial(_block_mean_kernel, group=group),
        grid=(batch * nblk // group,),
        in_specs=[pl.BlockSpec((group * MOBA_BLOCK, h_mb * hd), lambda i: (i, c_kmb // h_mb))],
        out_specs=pl.BlockSpec((group, h_mb * hd), lambda i: (i, 0)),
        out_shape=jax.ShapeDtypeStruct((batch * nblk, h_mb * hd), F32),
        compiler_params=_params("parallel"),
        name="moba_block_mean",
    )(proj)
    kmean = kmean.reshape(batch, nblk, h_mb, hd).transpose(0, 2, 1, 3)
    kmean = jnp.pad(kmean, ((0, 0), (0, 0), (0, hd - nblk), (0, 0)))

    blocks_per_step = MOBA_KEYS // MOBA_BLOCK
    o_mb = pl.pallas_call(
        functools.partial(_moba_prompt_kernel, n_top=min(MOBA_TOPK, nblk), n_blk=nblk, blocks_per_step=blocks_per_step),
        grid=(batch, h_mb, nblk),
        in_specs=[pl.BlockSpec(memory_space=pltpu.SMEM),
                  pl.BlockSpec((MOBA_BLOCK, hd), lambda b, h, i: (b * nblk + i, c_qmb + h)),
                  pl.BlockSpec((seq, hd), lambda b, h, i: (b, c_kmb + h)),
                  pl.BlockSpec((seq, hd), lambda b, h, i: (b, c_vmb + h)),
                  pl.BlockSpec((None, None, hd, hd), lambda b, h, i: (b, h, 0, 0)),
                  pl.BlockSpec((None, 1, hd), lambda b, h, i: (layer, 0, h_sb + h))],
        out_specs=pl.BlockSpec((MOBA_BLOCK, hd), lambda b, h, i: (b * nblk + i, h)),
        out_shape=jax.ShapeDtypeStruct((rows, h_mb * hd), BF16),
        scratch_shapes=[pltpu.VMEM((MOBA_BLOCK, hd), F32), pltpu.VMEM((MOBA_BLOCK, 1), F32),
                        pltpu.VMEM((MOBA_BLOCK, 1), F32), pltpu.VMEM((MOBA_BLOCK, hd), BF16),
                        pltpu.VMEM((MOBA_BLOCK, hd), F32), pltpu.VMEM((MOBA_BLOCK, MOBA_BLOCK), F32)],
        compiler_params=_params("parallel", "parallel", "arbitrary"),
        name="prompt_moba_attn",
    )(slopes, proj, proj, proj, kmean, gain)

    tq = min(512, seq)
    nq = seq // tq
    o_mem = pl.pallas_call(
        _mem_prompt_kernel,
        grid=(batch, h_mem, nq),
        in_specs=[pl.BlockSpec((tq, hd), lambda b, h, i: (b * nq + i, c_qmem + h)),
                  pl.BlockSpec((mem_len, hd), lambda b, h, i: (b, h)),
                  pl.BlockSpec((mem_len, hd), lambda b, h, i: (b, h_mem + h)),
                  pl.BlockSpec((None, 1, hd), lambda b, h, i: (layer, 0, h_sb + h_mb + h))],
        out_specs=pl.BlockSpec((tq, hd), lambda b, h, i: (b * nq + i, h)),
        out_shape=jax.ShapeDtypeStruct((rows, h_mem * hd), BF16),
        compiler_params=_params("parallel", "parallel", "parallel"),
        name="prompt_mem_attn",
    )(proj, memkv, memkv, gain)
    return jnp.concatenate([o_sb, o_mb, o_mem], axis=1)


def _kv_rows_kernel(*refs, n):
    for src_ref, dst_ref in zip(refs[:n], refs[2 * n:]):
        dst_ref[...] = src_ref[...]


def _store_kv_rows(proj, bufs, col_blocks, layer, batch, seq, n_heads):
    n = len(bufs)
    hd = HEAD_DIM
    tt = min(1024, seq)
    n_t = seq // tt
    return pl.pallas_call(
        functools.partial(_kv_rows_kernel, n=n),
        grid=(batch, n_heads, n_t),
        in_specs=[pl.BlockSpec((tt, hd), functools.partial(lambda b, h, i, c: (b * n_t + i, c + h), c=c))
                  for c in col_blocks] + [pl.BlockSpec(memory_space=pl.ANY)] * n,
        out_specs=[pl.BlockSpec((None, None, None, tt, hd), lambda b, h, i: (layer, b, h, i, 0))] * n,
        out_shape=[jax.ShapeDtypeStruct(buf.shape, buf.dtype) for buf in bufs],
        input_output_aliases={n + k: k for k in range(n)},
        compiler_params=_params("parallel", "parallel", "parallel"),
        name="store_kv_rows",
    )(*([proj] * n), *bufs)


def _dot3(x, w):
    xh, xl = _split(x)
    wh, wl = _split(w)
    m = x.shape[0]
    a = jnp.dot(jnp.concatenate([xh, xl], axis=0), wh, preferred_element_type=F32)
    return a[:m] + a[m:] + jnp.dot(xh, wl, preferred_element_type=F32)


def _dot3_nt(x, k):
    xh, xl = _split(x)
    kh, kl = _split(k)
    m = x.shape[0]
    a = _dot_nt(jnp.concatenate([xh, xl], axis=0), kh)
    return a[:m] + a[m:] + _dot_nt(xh, kl)


def _proj3_kernel(x_ref, w_ref, o_ref):
    o_ref[...] = _dot3(x_ref[...], w_ref[...])


def _proj3(x, w_stack, layer, tn):
    m, k = x.shape
    n = w_stack.shape[2]
    tn = min(tn, n)
    return pl.pallas_call(
        _proj3_kernel,
        grid=(n // tn,),
        in_specs=[pl.BlockSpec((m, k), lambda j: (0, 0)),
                  pl.BlockSpec((None, k, tn), lambda j: (layer, 0, j))],
        out_specs=pl.BlockSpec((m, tn), lambda j: (0, j)),
        out_shape=jax.ShapeDtypeStruct((m, n), F32),
        compiler_params=_params("parallel"),
        name="sample_proj",
    )(x, w_stack)


def _out_ln3_kernel(x_ref, w_ref, r_ref, g_ref, b_ref, o_ref, acc_ref, *, alpha):
    k = pl.program_id(0)

    @pl.when(k == 0)
    def _():
        acc_ref[...] = jnp.zeros_like(acc_ref)

    acc_ref[...] += _dot3(x_ref[...], w_ref[...])

    @pl.when(k == pl.num_programs(0) - 1)
    def _():
        o_ref[...] = _layer_norm(alpha * r_ref[...] + acc_ref[...], g_ref[...], b_ref[...])


def _out_ln3(x, w_stack, res, g, b, layer, alpha, tk):
    m, kdim = x.shape
    n = w_stack.shape[2]
    tk = min(tk, kdim)
    return pl.pallas_call(
        functools.partial(_out_ln3_kernel, alpha=alpha),
        grid=(kdim // tk,),
        in_specs=[pl.BlockSpec((m, tk), lambda k: (0, k)),
                  pl.BlockSpec((None, tk, n), lambda k: (layer, k, 0)),
                  pl.BlockSpec((m, n), lambda k: (0, 0)),
                  pl.BlockSpec((None, 1, n), lambda k: (layer, 0, 0)),
                  pl.BlockSpec((None, 1, n), lambda k: (layer, 0, 0))],
        out_specs=pl.BlockSpec((m, n), lambda k: (0, 0)),
        out_shape=jax.ShapeDtypeStruct((m, n), F32),
        scratch_shapes=[pltpu.VMEM((m, n), F32)],
        compiler_params=_params("arbitrary"),
        name="sample_out_ln",
    )(x, w_stack, res, g, b)


def _mlp3_kernel(x_ref, wu_ref, wd_ref, g_ref, b_ref, o_ref, acc_ref, *, alpha):
    f = pl.program_id(0)

    @pl.when(f == 0)
    def _():
        acc_ref[...] = jnp.zeros_like(acc_ref)

    h = jnp.maximum(_dot3(x_ref[...], wu_ref[...]), 0.0)
    acc_ref[...] += _dot3(h * h, wd_ref[...])

    @pl.when(f == pl.num_programs(0) - 1)
    def _():
        o_ref[...] = _layer_norm(alpha * x_ref[...] + acc_ref[...], g_ref[...], b_ref[...])


def _mlp3(x, wu_stack, wd_stack, g, b, layer, alpha, tf):
    m, d = x.shape
    ff = wu_stack.shape[2]
    tf = min(tf, ff)
    return pl.pallas_call(
        functools.partial(_mlp3_kernel, alpha=alpha),
        grid=(ff // tf,),
        in_specs=[pl.BlockSpec((m, d), lambda f: (0, 0)),
                  pl.BlockSpec((None, d, tf), lambda f: (layer, 0, f)),
                  pl.BlockSpec((None, tf, d), lambda f: (layer, f, 0)),
                  pl.BlockSpec((None, 1, d), lambda f: (layer, 0, 0)),
                  pl.BlockSpec((None, 1, d), lambda f: (layer, 0, 0))],
        out_specs=pl.BlockSpec((m, d), lambda f: (0, 0)),
        out_shape=jax.ShapeDtypeStruct((m, d), F32),
        scratch_shapes=[pltpu.VMEM((m, d), F32)],
        compiler_params=_params("arbitrary"),
        name="sample_mlp",
    )(x, wu_stack, wd_stack, g, b)


def _pick_rows(parts):
    row = lax.broadcasted_iota(jnp.int32, parts[0].shape, 0)
    out = jnp.zeros_like(parts[0])
    for h, part in enumerate(parts):
        out = jnp.where(row == h, part, out)
    return out


def _own_head_blocks(x, n_heads):
    return _pick_rows([x[:, h * HEAD_DIM:(h + 1) * HEAD_DIM] for h in range(n_heads)])


def _head_scores(q, keys):
    row = lax.broadcasted_iota(jnp.int32, q.shape, 0)
    q_bd = jnp.concatenate([jnp.where(row == h, q, 0.0) for h in range(len(keys))], axis=1)
    return _dot3_nt(q_bd, jnp.concatenate(keys, axis=1)) * ATTN_SCALE


def _head_values(w, values):
    return _dot3(w, jnp.concatenate(values, axis=1))


def _suffix_sums3(lk, uu):
    hi, lo = _split(lk)
    lo2 = (lk - hi.astype(F32) - lo.astype(F32)).astype(BF16)
    loc = jnp.dot(jnp.concatenate([hi, lo], axis=1), uu, preferred_element_type=F32)
    loc = loc + jnp.dot(lo2, uu[:SUFFIX_CHUNK], preferred_element_type=F32)
    return loc, loc[:, 0:1] + lk[:, 0:1]


def _sb_decode_kernel(pt_ref, q_ref, *refs, n_heads, n_sub):
    k_refs, v_refs = refs[:n_sub], refs[n_sub:2 * n_sub]
    uu_ref, g_ref, o_ref, acc_ref, carry_ref = refs[2 * n_sub:]
    j = pl.program_id(1)

    @pl.when(j == 0)
    def _():
        acc_ref[...] = jnp.zeros_like(acc_ref)
        carry_ref[...] = jnp.zeros_like(carry_ref)

    def tokens(page_refs, h):
        return jnp.concatenate([r[h] for r in page_refs], axis=0)

    z = _head_scores(q_ref[...], [tokens(k_refs, h) for h in range(n_heads)])
    nsp = _neg_softplus(z)
    uu = uu_ref[...]
    run = carry_ref[...]
    n_chunks = z.shape[1] // SUFFIX_CHUNK
    between = [None] * n_chunks
    for c in reversed(range(n_chunks)):
        loc, tot = _suffix_sums3(nsp[:, c * SUFFIX_CHUNK:(c + 1) * SUFFIX_CHUNK], uu)
        between[c] = loc + run
        run = run + tot
    w = jnp.exp(z + nsp + jnp.concatenate(between, axis=1))
    acc_ref[...] += _head_values(w, [tokens(v_refs, h) for h in range(n_heads)])
    carry_ref[...] = run

    @pl.when(j == pl.num_programs(1) - 1)
    def _():
        o_ref[...] = _head_rms(_own_head_blocks(acc_ref[...], n_heads), g_ref[...])


def _moba_gate_kernel(pt_ref, q_ref, *refs, n_top, n_sub, pages_per_block):
    k_refs = refs[:n_sub * pages_per_block]
    idx_ref, gate_ref = refs[n_sub * pages_per_block:]
    n = pl.program_id(1)

    @pl.when(n == 0)
    def _():
        gate_ref[...] = jnp.full_like(gate_ref, NEG_INF)

    n_heads = k_refs[0].shape[0]
    lane = lax.broadcasted_iota(jnp.int32, (n_heads, HEAD_DIM), 1)
    gate = gate_ref[0:n_heads, :]
    for b in range(n_sub):
        pages = k_refs[b * pages_per_block:(b + 1) * pages_per_block]
        mean = sum(jnp.sum(r[...], axis=1) for r in pages) * (1.0 / MOBA_BLOCK)
        g = jnp.sum(q_ref[0:n_heads, :] * mean, axis=-1, keepdims=True)
        gate = jnp.where(lane == n * n_sub + b, g, gate)
    gate_ref[0:n_heads, :] = gate

    @pl.when(n == pl.num_programs(1) - 1)
    def _():
        gate = gate_ref[...]
        lanes = lax.broadcasted_iota(jnp.int32, gate.shape, 1)
        idx = jnp.zeros(gate.shape, jnp.int32)
        for r in range(n_top):
            best = jnp.max(gate, axis=-1, keepdims=True)
            first = jnp.min(jnp.where(gate == best, lanes, HEAD_DIM), axis=-1, keepdims=True)
            idx = jnp.where(lanes == r, first, idx)
            gate = jnp.where(lanes == first, NEG_INF, gate)
        idx_ref[...] = idx


def _moba_decode_kernel(pt_ref, top_ref, slope_ref, q_ref, kn_ref, vn_ref, g_ref, *refs, n_heads, past_len, page):
    k_refs, v_refs = refs[:n_heads], refs[n_heads:2 * n_heads]
    o_ref, acc_ref, m_ref, l_ref = refs[2 * n_heads:]
    s_id, t = pl.program_id(0), pl.program_id(1)
    pages_per_block = MOBA_BLOCK // page
    q = q_ref[...]

    @pl.when(t == 0)
    def _():
        m_ref[...] = jnp.sum(q * kn_ref[...], axis=-1, keepdims=True) * ATTN_SCALE
        l_ref[...] = jnp.ones_like(l_ref)
        acc_ref[...] = vn_ref[...]

    row = lax.broadcasted_iota(jnp.int32, (8, 1), 0)
    slope = jnp.zeros((8, 1), F32)
    dist0 = jnp.zeros((8, 1), F32)
    for h in range(n_heads):
        kpos0 = top_ref[s_id, h, t // pages_per_block] * MOBA_BLOCK + (t % pages_per_block) * page
        slope = jnp.where(row == h, slope_ref[h], slope)
        dist0 = jnp.where(row == h, (past_len - kpos0).astype(F32), dist0)
    lane = lax.broadcasted_iota(jnp.int32, (8, page), 1)
    s = _head_scores(q, [r[...] for r in k_refs]) - slope * (dist0 - lane.astype(F32))
    m_old = m_ref[...]
    m_new = jnp.maximum(m_old, jnp.max(s, axis=-1, keepdims=True))
    a = jnp.exp(m_old - m_new)
    e = jnp.exp(s - m_new)
    l_ref[...] = a * l_ref[...] + jnp.sum(e, axis=-1, keepdims=True)
    acc_ref[...] = a * acc_ref[...] + _own_head_blocks(_head_values(e, [r[...] for r in v_refs]), n_heads)
    m_ref[...] = m_new

    @pl.when(t == pl.num_programs(1) - 1)
    def _():
        o_ref[...] = _head_rms(acc_ref[...] / l_ref[...], g_ref[...])


def _mem_decode_kernel(q_ref, k_ref, v_ref, g_ref, o_ref, *, n_heads):
    s = _head_scores(q_ref[...], [k_ref[:, h, :] for h in range(n_heads)])
    e = jnp.exp(s - jnp.max(s, axis=-1, keepdims=True))
    o = _head_values(e, [v_ref[:, h, :] for h in range(n_heads)])
    o = _own_head_blocks(o, n_heads) / jnp.sum(e, axis=-1, keepdims=True)
    o_ref[...] = _head_rms(o, g_ref[...])


def _heads_to_rows(x, n_heads):
    x = x.reshape(x.shape[0], n_heads, HEAD_DIM)
    return jnp.pad(x, ((0, 0), (0, 8 - n_heads), (0, 0)))


def _pages_per_step(n, most):
    return max(d for d in range(1, most + 1) if n % d == 0)


def _sample_attention(proj, gain_l, page_table, pools, mem_k, mem_v, slopes, uu, layer, h_sb, h_mb, h_mem):
    cache_k_sb, cache_v_sb, cache_k_mb, cache_v_mb = pools
    n_seq, n_pages = page_table.shape
    page = cache_k_sb.shape[3]
    past_len = n_pages * page
    hd = HEAD_DIM
    w_sb, w_mb = h_sb * hd, h_mb * hd
    q_sb = _heads_to_rows(proj[:, 0:w_sb], h_sb)
    q_mb = _heads_to_rows(proj[:, 3 * w_sb:3 * w_sb + w_mb], h_mb)
    k_mb_new = _heads_to_rows(proj[:, 3 * w_sb + w_mb:3 * w_sb + 2 * w_mb], h_mb)
    v_mb_new = _heads_to_rows(proj[:, 3 * w_sb + 2 * w_mb:3 * w_sb + 3 * w_mb], h_mb)
    q_mem = _heads_to_rows(proj[:, 3 * w_sb + 3 * w_mb:], h_mem)
    g_sb = _heads_to_rows(gain_l[None, 0:w_sb], h_sb)[0]
    g_mb = _heads_to_rows(gain_l[None, w_sb:w_sb + w_mb], h_mb)[0]
    g_mem = _heads_to_rows(gain_l[None, w_sb + w_mb:], h_mem)[0]

    row_spec = pl.BlockSpec((None, 8, hd), lambda s, j, *_: (s, 0, 0))
    gain_spec = pl.BlockSpec((8, hd), lambda s, j, *_: (0, 0))
    row_shape = jax.ShapeDtypeStruct((n_seq, 8, hd), F32)

    def page_spec(n_heads, page_of):
        return pl.BlockSpec((None, None, n_heads, page, hd),
                            lambda s, j, pt: (layer, pt[s, page_of(j)], 0, 0, 0))

    n_sub = _pages_per_step(n_pages, 8)
    sb_pages = [page_spec(h_sb, functools.partial(lambda j, i: n_pages - n_sub * (j + 1) + i, i=i))
                for i in range(n_sub)]
    o_sb = pl.pallas_call(
        functools.partial(_sb_decode_kernel, n_heads=h_sb, n_sub=n_sub),
        grid_spec=pltpu.PrefetchScalarGridSpec(
            num_scalar_prefetch=1, grid=(n_seq, n_pages // n_sub),
            in_specs=[row_spec] + sb_pages + sb_pages
                     + [pl.BlockSpec((2 * SUFFIX_CHUNK, SUFFIX_CHUNK), lambda s, j, pt: (0, 0)), gain_spec],
            out_specs=row_spec,
            scratch_shapes=[pltpu.VMEM((8, h_sb * hd), F32), pltpu.VMEM((8, 1), F32)]),
        out_shape=row_shape,
        compiler_params=_params("parallel", "arbitrary"),
        name="sample_sb_attn",
    )(page_table, q_sb, *([cache_k_sb] * n_sub), *([cache_v_sb] * n_sub), uu, g_sb)

    pages_per_block = MOBA_BLOCK // page
    n_past_blocks = past_len // MOBA_BLOCK
    n_top = min(MOBA_TOPK, n_past_blocks)
    n_gate = _pages_per_step(n_past_blocks, 8)
    gate_pages = [page_spec(h_mb, functools.partial(lambda n, i: n_gate * pages_per_block * n + i, i=i))
                  for i in range(n_gate * pages_per_block)]
    top, _ = pl.pallas_call(
        functools.partial(_moba_gate_kernel, n_top=n_top, n_sub=n_gate, pages_per_block=pages_per_block),
        grid_spec=pltpu.PrefetchScalarGridSpec(
            num_scalar_prefetch=1, grid=(n_seq, n_past_blocks // n_gate),
            in_specs=[row_spec] + gate_pages,
            out_specs=[row_spec, row_spec]),
        out_shape=[jax.ShapeDtypeStruct((n_seq, 8, hd), jnp.int32), row_shape],
        compiler_params=_params("parallel", "arbitrary"),
        name="sample_moba_gate",
    )(page_table, q_mb, *([cache_k_mb] * (n_gate * pages_per_block)))
    top = top[:, :, :max(n_top, 1)]

    def sel_page(h):
        return pl.BlockSpec(
            (None, None, None, page, hd),
            lambda s, t, pt, tp: (layer, pt[s, tp[s, h, t // pages_per_block] * pages_per_block + t % pages_per_block],
                                  h, 0, 0))

    sel_pages = [sel_page(h) for h in range(h_mb)]
    o_mb = pl.pallas_call(
        functools.partial(_moba_decode_kernel, n_heads=h_mb, past_len=past_len, page=page),
        grid_spec=pltpu.PrefetchScalarGridSpec(
            num_scalar_prefetch=2, grid=(n_seq, n_top * pages_per_block),
            in_specs=[pl.BlockSpec(memory_space=pltpu.SMEM), row_spec, row_spec, row_spec, gain_spec]
                     + sel_pages + sel_pages,
            out_specs=row_spec,
            scratch_shapes=[pltpu.VMEM((8, hd), F32), pltpu.VMEM((8, 1), F32), pltpu.VMEM((8, 1), F32)]),
        out_shape=row_shape,
        compiler_params=_params("parallel", "arbitrary"),
        name="sample_moba_attn",
    )(page_table, top, slopes, q_mb, k_mb_new, v_mb_new, g_mb, *([cache_k_mb] * h_mb), *([cache_v_mb] * h_mb))

    mem_len = mem_k.shape[2]
    mem_spec = pl.BlockSpec((None, None, mem_len, h_mem, hd), lambda s: (layer, s, 0, 0, 0))
    o_mem = pl.pallas_call(
        functools.partial(_mem_decode_kernel, n_heads=h_mem),
        grid=(n_seq,),
        in_specs=[pl.BlockSpec((None, 8, hd), lambda s: (s, 0, 0)), mem_spec, mem_spec,
                  pl.BlockSpec((8, hd), lambda s: (0, 0))],
        out_specs=pl.BlockSpec((None, 8, hd), lambda s: (s, 0, 0)),
        out_shape=row_shape,
        compiler_params=_params("parallel"),
        name="sample_mem_attn",
    )(q_mem, mem_k, mem_v, g_mem)

    return jnp.concatenate([o_sb[:, :h_sb].reshape(n_seq, w_sb), o_mb[:, :h_mb].reshape(n_seq, w_mb),
                            o_mem[:, :h_mem].reshape(n_seq, h_mem * hd)], axis=1)


def kernel(x_prompt, x_sample, mem_prompt, cache_k_sb, cache_v_sb, cache_k_moba, cache_v_moba, cache_mem_k, cache_mem_v, page_table, w_in, w_mem_kv, out_norm_g, w_o, ln1_g, ln1_b, w_up, w_down, ln2_g, ln2_b):
    batch, seq, d_model = x_prompt.shape
    n_seq, dec_seq, _ = x_sample.shape
    depth = w_in.shape[0]
    h_sb, h_mb, h_mem = cache_k_sb.shape[3], cache_k_moba.shape[3], cache_mem_k.shape[3]
    mem_len = mem_prompt.shape[1]
    hd = HEAD_DIM
    assert dec_seq == 1 and cache_k_sb.shape[4] == hd and seq % MOBA_KEYS == 0
    assert MOBA_BLOCK % cache_k_sb.shape[2] == 0 and seq // MOBA_BLOCK <= hd
    w_mem = h_mem * hd
    alpha = (2 * depth) ** 0.25

    w_in_b, w_kv_b, w_o_b = w_in.astype(BF16), w_mem_kv.astype(BF16), w_o.astype(BF16)
    w_up_b, w_down_b = w_up.astype(BF16), w_down.astype(BF16)
    gain = out_norm_g.reshape(depth, 1, d_model)
    ln1_g3, ln1_b3 = ln1_g.reshape(depth, 1, d_model), ln1_b.reshape(depth, 1, d_model)
    ln2_g3, ln2_b3 = ln2_g.reshape(depth, 1, d_model), ln2_b.reshape(depth, 1, d_model)
    slopes = jnp.exp2(-8.0 * jnp.arange(1, h_mb + 1, dtype=F32) / h_mb)
    uu = _suffix_matrix()
    pools = tuple(jnp.transpose(c, (0, 1, 3, 2, 4)) for c in (cache_k_sb, cache_v_sb, cache_k_moba, cache_v_moba))

    hp = x_prompt.reshape(batch * seq, d_model)
    hs = x_sample.reshape(n_seq, d_model)
    mem2d = mem_prompt.reshape(batch * mem_len, d_model)
    assert h_sb == h_mb
    kv_p = [jnp.zeros((depth, batch, h_sb, seq, hd), F32) for _ in range(4)]
    kv_cols = (h_sb, 2 * h_sb, 3 * h_sb + h_mb, 3 * h_sb + 2 * h_mb)
    outs = [[] for _ in range(10)]
    for l in range(depth):
        memkv = _proj(mem2d, w_kv_b, l, 512, 512)
        proj_p = _proj(hp, w_in_b, l, 1024, 1024)
        mix_p = _prompt_attention(proj_p, memkv, gain, slopes, uu, l, batch, seq, h_sb, h_mb, h_mem, mem_len)
        hp = _out_ln(mix_p, w_o_b, hp, ln1_g3, ln1_b3, l, alpha, 1024, 512)
        hp = _mlp(hp, w_up_b, w_down_b, ln2_g3, ln2_b3, l, alpha, 512, 1024)

        proj_s = _proj3(hs, w_in, l, 512)
        mix_s = _sample_attention(proj_s, out_norm_g[l], page_table, pools, cache_mem_k, cache_mem_v,
                                  slopes, uu, l, h_sb, h_mb, h_mem)
        hs = _out_ln3(mix_s, w_o, hs, ln1_g3, ln1_b3, l, alpha, 512)
        hs = _mlp3(hs, w_up, w_down, ln2_g3, ln2_b3, l, alpha, 512)

        kv_p = _store_kv_rows(proj_p, kv_p, kv_cols, l, batch, seq, h_sb)
        for i, c in enumerate(kv_cols):
            outs[6 + i].append(proj_s[:, c * hd:(c + h_sb) * hd].reshape(n_seq, 1, h_sb, hd))
        outs[4].append(memkv[:, :w_mem].reshape(batch, mem_len, h_mem, hd))
        outs[5].append(memkv[:, w_mem:].reshape(batch, mem_len, h_mem, hd))
    return (hp.reshape(batch, seq, d_model), hs.reshape(n_seq, 1, d_model),
            *[jnp.transpose(buf, (0, 1, 3, 2, 4)) for buf in kv_p],
            *[jnp.stack(o) for o in outs[4:]])
```

```python
import functools

import numpy as np
import jax
import jax.numpy as jnp
from jax import lax
from jax.experimental import pallas as pl
from jax.experimental.pallas import tpu as pltpu

F32 = jnp.float32
BF16 = jnp.bfloat16

HEAD_DIM = 128
MOBA_BLOCK = 256
MOBA_TOPK = 3
LN_EPS = 1e-5
RMS_EPS = 1e-6
NEG_INF = -1e30
ATTN_SCALE = HEAD_DIM ** -0.5
SB_TILE = 512
SB_HEADS_PER_STEP = 2
MOBA_KEYS = 4 * MOBA_BLOCK
LOG2E = 1.4426950408889634
SUFFIX_CHUNK = 128
VMEM_LIMIT = 56 * 1024 * 1024


def _params(*sem):
    return pltpu.CompilerParams(dimension_semantics=sem, vmem_limit_bytes=VMEM_LIMIT)


def _dot_nt(a, b):
    return lax.dot_general(a, b, (((1,), (1,)), ((), ())), preferred_element_type=F32)


def _head_rms(o, g):
    return o * lax.rsqrt(jnp.mean(o * o, axis=-1, keepdims=True) + RMS_EPS) * g


def _layer_norm(y, g, b):
    mu = jnp.mean(y, axis=-1, keepdims=True)
    d = y - mu
    var = jnp.mean(d * d, axis=-1, keepdims=True)
    return d * lax.rsqrt(var + LN_EPS) * g + b


def _neg_softplus(z):
    return -(jnp.maximum(z, 0.0) + jnp.log(1.0 + jnp.exp(-jnp.abs(z))))


def _split(x):
    hi = x.astype(BF16)
    return hi, (x - hi.astype(F32)).astype(BF16)


def _suffix_matrix():
    j = np.arange(SUFFIX_CHUNK)
    u = (j[:, None] > j[None, :]).astype(np.float32)
    return jnp.asarray(np.concatenate([u, u], axis=0), dtype=BF16)


def _proj_kernel(x_ref, w_ref, o_ref, xb_ref):
    @pl.when(pl.program_id(1) == 0)
    def _():
        xb_ref[...] = x_ref[...].astype(BF16)

    o_ref[...] = jnp.dot(xb_ref[...], w_ref[...], preferred_element_type=F32)


def _proj(x, w_stack, layer, tm, tn):
    m, k = x.shape
    n = w_stack.shape[2]
    tm, tn = min(tm, m), min(tn, n)
    return pl.pallas_call(
        _proj_kernel,
        grid=(m // tm, n // tn),
        in_specs=[pl.BlockSpec((tm, k), lambda i, j: (i, 0)),
                  pl.BlockSpec((None, k, tn), lambda i, j: (layer, 0, j))],
        out_specs=pl.BlockSpec((tm, tn), lambda i, j: (i, j)),
        out_shape=jax.ShapeDtypeStruct((m, n), F32),
        scratch_shapes=[pltpu.VMEM((tm, k), BF16)],
        compiler_params=_params("parallel", "arbitrary"),
        name="proj",
    )(x, w_stack)


def _out_ln_kernel(x_ref, w_ref, r_ref, g_ref, b_ref, o_ref, acc_ref, *, alpha):
    k = pl.program_id(1)

    @pl.when(k == 0)
    def _():
        acc_ref[...] = jnp.zeros_like(acc_ref)

    acc_ref[...] += jnp.dot(x_ref[...], w_ref[...], preferred_element_type=F32)

    @pl.when(k == pl.num_programs(1) - 1)
    def _():
        o_ref[...] = _layer_norm(alpha * r_ref[...] + acc_ref[...], g_ref[...], b_ref[...])


def _out_ln(x, w_stack, res, g, b, layer, alpha, tm, tk):
    m, kdim = x.shape
    n = w_stack.shape[2]
    tm, tk = min(tm, m), min(tk, kdim)
    return pl.pallas_call(
        functools.partial(_out_ln_kernel, alpha=alpha),
        grid=(m // tm, kdim // tk),
        in_specs=[pl.BlockSpec((tm, tk), lambda i, k: (i, k)),
                  pl.BlockSpec((None, tk, n), lambda i, k: (layer, k, 0)),
                  pl.BlockSpec((tm, n), lambda i, k: (i, 0)),
                  pl.BlockSpec((None, 1, n), lambda i, k: (layer, 0, 0)),
                  pl.BlockSpec((None, 1, n), lambda i, k: (layer, 0, 0))],
        out_specs=pl.BlockSpec((tm, n), lambda i, k: (i, 0)),
        out_shape=jax.ShapeDtypeStruct((m, n), F32),
        scratch_shapes=[pltpu.VMEM((tm, n), F32)],
        compiler_params=_params("parallel", "arbitrary"),
        name="out_ln",
    )(x, w_stack, res, g, b)


def _mlp_kernel(x_ref, wu_ref, wd_ref, g_ref, b_ref, o_ref, acc_ref, xb_ref, *, alpha):
    f = pl.program_id(1)

    @pl.when(f == 0)
    def _():
        xb_ref[...] = x_ref[...].astype(BF16)
        acc_ref[...] = jnp.zeros_like(acc_ref)

    h = jnp.maximum(jnp.dot(xb_ref[...], wu_ref[...], preferred_element_type=F32), 0.0)
    acc_ref[...] += jnp.dot((h * h).astype(BF16), wd_ref[...], preferred_element_type=F32)

    @pl.when(f == pl.num_programs(1) - 1)
    def _():
        o_ref[...] = _layer_norm(alpha * x_ref[...] + acc_ref[...], g_ref[...], b_ref[...])


def _mlp(x, wu_stack, wd_stack, g, b, layer, alpha, tm, tf):
    m, d = x.shape
    ff = wu_stack.shape[2]
    tm, tf = min(tm, m), min(tf, ff)
    return pl.pallas_call(
        functools.partial(_mlp_kernel, alpha=alpha),
        grid=(m // tm, ff // tf),
        in_specs=[pl.BlockSpec((tm, d), lambda i, f: (i, 0)),
                  pl.BlockSpec((None, d, tf), lambda i, f: (layer, 0, f)),
                  pl.BlockSpec((None, tf, d), lambda i, f: (layer, f, 0)),
                  pl.BlockSpec((None, 1, d), lambda i, f: (layer, 0, 0)),
                  pl.BlockSpec((None, 1, d), lambda i, f: (layer, 0, 0))],
        out_specs=pl.BlockSpec((tm, d), lambda i, f: (i, 0)),
        out_shape=jax.ShapeDtypeStruct((m, d), F32),
        scratch_shapes=[pltpu.VMEM((tm, d), F32), pltpu.VMEM((tm, d), BF16)],
        compiler_params=_params("parallel", "arbitrary"),
        name="mlp",
    )(x, wu_stack, wd_stack, g, b)


def _sb_prompt_kernel(q_ref, k_ref, v_ref, uu_ref, g_ref, o_ref, acc_ref, carry_ref, qb_ref):
    qi = pl.program_id(2)
    t = q_ref.shape[0]
    heads = [slice(h * HEAD_DIM, (h + 1) * HEAD_DIM) for h in range(q_ref.shape[1] // HEAD_DIM)]
    chunks = [slice(c * SUFFIX_CHUNK, (c + 1) * SUFFIX_CHUNK) for c in range(t // SUFFIX_CHUNK)]
    acc_ref[...] = jnp.zeros_like(acc_ref)
    carry_ref[...] = jnp.zeros_like(carry_ref)
    qb_ref[...] = q_ref[...].astype(BF16)

    def tile(kj, diagonal):
        for h, cols in enumerate(heads):
            head_tile(kj, diagonal, h, cols)

    def head_tile(kj, diagonal, h, cols):
        rows = pl.ds(pl.multiple_of(kj * t, t), t)
        z = _dot_nt(qb_ref[:, cols], k_ref[rows, cols].astype(BF16)) * ATTN_SCALE
        nsp = _neg_softplus(z)
        if diagonal:
            past = lax.broadcasted_iota(jnp.int32, (t, t), 1) < lax.broadcasted_iota(jnp.int32, (t, t), 0)
            lk = jnp.where(past, nsp, 0.0)
        else:
            lk = nsp
        hi, lo = _split(lk)
        uu = uu_ref[...]
        locs = [jnp.dot(jnp.concatenate([hi[:, c], lo[:, c]], axis=1), uu, preferred_element_type=F32)
                for c in chunks]
        run = carry_ref[h]
        between = [None] * len(chunks)
        for i in reversed(range(len(chunks))):
            between[i] = locs[i] + run
            run = run + (locs[i][:, 0:1] + lk[:, chunks[i].start:chunks[i].start + 1])
        w = jnp.exp(z + nsp + jnp.concatenate(between, axis=1))
        if diagonal:
            w = jnp.where(past, w, 0.0)
        acc_ref[:, cols] += jnp.dot(w.astype(BF16), v_ref[rows, cols].astype(BF16), preferred_element_type=F32)
        carry_ref[h] = run

    tile(qi, True)

    def older(i, carry):
        tile(qi - 1 - i, False)
        return carry

    lax.fori_loop(0, qi, older, 0)
    for cols in heads:
        o_ref[:, cols] = _head_rms(acc_ref[:, cols], g_ref[:, cols]).astype(o_ref.dtype)


def _moba_prompt_kernel(slope_ref, q_ref, k_ref, v_ref, km_ref, g_ref, o_ref,
                        acc_ref, m_ref, l_ref, qb_ref, sel_ref, bias_ref, *, n_top, n_blk, blocks_per_step):
    h = pl.program_id(1)
    qi = pl.program_id(2)
    t = MOBA_BLOCK
    slope2 = slope_ref[h] * LOG2E
    own_tile = qi // blocks_per_step
    own_slot = qi % blocks_per_step

    acc_ref[...] = jnp.zeros_like(acc_ref)
    m_ref[...] = jnp.full_like(m_ref, NEG_INF)
    l_ref[...] = jnp.zeros_like(l_ref)
    qb_ref[...] = (q_ref[...] * (ATTN_SCALE * LOG2E)).astype(BF16)
    gate = lax.dot_general(km_ref[...], q_ref[...], (((1,), (1,)), ((), ())),
                           preferred_element_type=F32, precision=lax.Precision.HIGHEST)
    blk_row = lax.broadcasted_iota(jnp.int32, (HEAD_DIM, t), 0)
    sel_t = jnp.zeros((HEAD_DIM, t), F32)
    for n in range(n_blk):
        g_n = gate[n:n + 1, :]
        beats = ((gate > g_n) | ((gate == g_n) & (blk_row < n))) & (blk_row < qi)
        rank = jnp.sum(jnp.where(beats, 1.0, 0.0), axis=0, keepdims=True)
        sel_t = jnp.where(blk_row == n, jnp.where(rank < n_top, 1.0, 0.0), sel_t)
    sel_ref[...] = sel_t.T
    bias_ref[...] = (lax.broadcasted_iota(jnp.int32, (t, t), 1)
                     - lax.broadcasted_iota(jnp.int32, (t, t), 0)).astype(F32) * slope2

    def selected(blk):
        sel = sel_ref[...]
        lane = lax.broadcasted_iota(jnp.int32, sel.shape, 1)
        return jnp.sum(jnp.where(lane == blk, sel, 0.0), axis=-1, keepdims=True) > 0.5

    def update(kj, slots):
        rows = [pl.ds(pl.multiple_of(kj * (blocks_per_step * t) + s * t, t), t) for s, _, _ in slots]
        z = _dot_nt(qb_ref[...], jnp.concatenate([k_ref[r, :] for r in rows], axis=0).astype(BF16))
        bias = bias_ref[...]
        parts = []
        for i, (_, blk, kind) in enumerate(slots):
            s = (z[:, i * t:(i + 1) * t] - slope2 * ((qi - blk) * t).astype(F32)) + bias
            if kind == "own":
                mask = lax.broadcasted_iota(jnp.int32, (t, t), 1) <= lax.broadcasted_iota(jnp.int32, (t, t), 0)
            else:
                mask = selected(blk)
            parts.append(jnp.where(mask, s, NEG_INF))
        m_old = m_ref[...]
        m_new = m_old
        for s in parts:
            m_new = jnp.maximum(m_new, jnp.max(s, axis=-1, keepdims=True))
        a = jnp.exp2(m_old - m_new)
        es = [jnp.exp2(s - m_new) for s in parts]
        l_new = a * l_ref[...]
        for e in es:
            l_new = l_new + jnp.sum(e, axis=-1, keepdims=True)
        e_all = jnp.concatenate(es, axis=1).astype(BF16)
        v_all = jnp.concatenate([v_ref[r, :] for r in rows], axis=0).astype(BF16)
        acc_ref[...] = a * acc_ref[...] + jnp.dot(e_all, v_all, preferred_element_type=F32)
        l_ref[...] = l_new
        m_ref[...] = m_new

    for r in range(blocks_per_step):
        @pl.when(own_slot == r)
        def _(r=r):
            update(own_tile, [(r, qi, "own")] + [(i, qi - r + i, "past") for i in range(r)])

    def older(kj, carry):
        update(kj, [(i, blocks_per_step * kj + i, "past") for i in range(blocks_per_step)])
        return carry

    lax.fori_loop(0, own_tile, older, 0)
    o_ref[...] = _head_rms(acc_ref[...] / l_ref[...], g_ref[...]).astype(o_ref.dtype)


def _mem_prompt_kernel(q_ref, k_ref, v_ref, g_ref, o_ref):
    s = _dot_nt(q_ref[...].astype(BF16), k_ref[...].astype(BF16)) * ATTN_SCALE
    e = jnp.exp(s - jnp.max(s, axis=-1, keepdims=True))
    o = jnp.dot(e.astype(BF16), v_ref[...].astype(BF16), preferred_element_type=F32)
    o = o / jnp.sum(e, axis=-1, keepdims=True)
    o_ref[...] = _head_rms(o, g_ref[...]).astype(o_ref.dtype)


def _block_mean_kernel(k_ref, o_ref, *, group):
    x = k_ref[...]
    o_ref[...] = jnp.sum(x.reshape(group, MOBA_BLOCK, x.shape[-1]), axis=1) * (1.0 / MOBA_BLOCK)


def _prompt_attention(proj, memkv, gain, slopes, uu, layer, batch, seq, h_sb, h_mb, h_mem, mem_len):
    rows = batch * seq
    hd = HEAD_DIM
    c_qsb, c_ksb, c_vsb = 0, h_sb, 2 * h_sb
    c_qmb, c_kmb, c_vmb = 3 * h_sb, 3 * h_sb + h_mb, 3 * h_sb + 2 * h_mb
    c_qmem = 3 * h_sb + 3 * h_mb
    assert c_kmb % h_mb == 0

    t = min(SB_TILE, seq)
    n_t = seq // t
    hps = SB_HEADS_PER_STEP
    assert h_sb % hps == 0 and c_ksb % hps == 0 and c_vsb % hps == 0
    wd = hps * hd
    o_sb = pl.pallas_call(
        _sb_prompt_kernel,
        grid=(batch, h_sb // hps, n_t),
        in_specs=[pl.BlockSpec((t, wd), lambda b, h, i: (b * n_t + i, c_qsb // hps + h)),
                  pl.BlockSpec((seq, wd), lambda b, h, i: (b, c_ksb // hps + h)),
                  pl.BlockSpec((seq, wd), lambda b, h, i: (b, c_vsb // hps + h)),
                  pl.BlockSpec((2 * SUFFIX_CHUNK, SUFFIX_CHUNK), lambda b, h, i: (0, 0)),
                  pl.BlockSpec((None, 1, wd), lambda b, h, i: (layer, 0, h))],
        out_specs=pl.BlockSpec((t, wd), lambda b, h, i: (b * n_t + i, h)),
        out_shape=jax.ShapeDtypeStruct((rows, h_sb * hd), BF16),
        scratch_shapes=[pltpu.VMEM((t, wd), F32), pltpu.VMEM((hps, t, 1), F32), pltpu.VMEM((t, wd), BF16)],
        compiler_params=_params("parallel", "parallel", "arbitrary"),
        name="prompt_sb_attn",
    )(proj, proj, proj, uu, gain)

    nblk = seq // MOBA_BLOCK
    group = min(8, batch * nblk)
    kmean = pl.pallas_call(
        functools.partial(_block_mean_kernel, group=group),
        grid=(batch * nblk // group,),
        in_specs=[pl.BlockSpec((group * MOBA_BLOCK, h_mb * hd), lambda i: (i, c_kmb // h_mb))],
        out_specs=pl.BlockSpec((group, h_mb * hd), lambda i: (i, 0)),
        out_shape=jax.ShapeDtypeStruct((batch * nblk, h_mb * hd), F32),
        compiler_params=_params("parallel"),
        name="moba_block_mean",
    )(proj)
    kmean = kmean.reshape(batch, nblk, h_mb, hd).transpose(0, 2, 1, 3)
    kmean = jnp.pad(kmean, ((0, 0), (0, 0), (0, hd - nblk), (0, 0)))

    blocks_per_step = MOBA_KEYS // MOBA_BLOCK
    o_mb = pl.pallas_call(
        functools.partial(_moba_prompt_kernel, n_top=min(MOBA_TOPK, nblk), n_blk=nblk, blocks_per_step=blocks_per_step),
        grid=(batch, h_mb, nblk),
        in_specs=[pl.BlockSpec(memory_space=pltpu.SMEM),
                  pl.BlockSpec((MOBA_BLOCK, hd), lambda b, h, i: (b * nblk + i, c_qmb + h)),
                  pl.BlockSpec((seq, hd), lambda b, h, i: (b, c_kmb + h)),
                  pl.BlockSpec((seq, hd), lambda b, h, i: (b, c_vmb + h)),
                  pl.BlockSpec((None, None, hd, hd), lambda b, h, i: (b, h, 0, 0)),
                  pl.BlockSpec((None, 1, hd), lambda b, h, i: (layer, 0, h_sb + h))],
        out_specs=pl.BlockSpec((MOBA_BLOCK, hd), lambda b, h, i: (b * nblk + i, h)),
        out_shape=jax.ShapeDtypeStruct((rows, h_mb * hd), BF16),
        scratch_shapes=[pltpu.VMEM((MOBA_BLOCK, hd), F32), pltpu.VMEM((MOBA_BLOCK, 1), F32),
                        pltpu.VMEM((MOBA_BLOCK, 1), F32), pltpu.VMEM((MOBA_BLOCK, hd), BF16),
                        pltpu.VMEM((MOBA_BLOCK, hd), F32), pltpu.VMEM((MOBA_BLOCK, MOBA_BLOCK), F32)],
        compiler_params=_params("parallel", "parallel", "arbitrary"),
        name="prompt_moba_attn",
    )(slopes, proj, proj, proj, kmean, gain)

    tq = min(512, seq)
    nq = seq // tq
    o_mem = pl.pallas_call(
        _mem_prompt_kernel,
        grid=(batch, h_mem, nq),
        in_specs=[pl.BlockSpec((tq, hd), lambda b, h, i: (b * nq + i, c_qmem + h)),
                  pl.BlockSpec((mem_len, hd), lambda b, h, i: (b, h)),
                  pl.BlockSpec((mem_len, hd), lambda b, h, i: (b, h_mem + h)),
                  pl.BlockSpec((None, 1, hd), lambda b, h, i: (layer, 0, h_sb + h_mb + h))],
        out_specs=pl.BlockSpec((tq, hd), lambda b, h, i: (b * nq + i, h)),
        out_shape=jax.ShapeDtypeStruct((rows, h_mem * hd), BF16),
        compiler_params=_params("parallel", "parallel", "parallel"),
        name="prompt_mem_attn",
    )(proj, memkv, memkv, gain)
    return jnp.concatenate([o_sb, o_mb, o_mem], axis=1)


def _kv_rows_kernel(*refs, n):
    for src_ref, dst_ref in zip(refs[:n], refs[2 * n:]):
        dst_ref[...] = src_ref[...]


def _store_kv_rows(proj, bufs, col_blocks, layer, batch, seq, n_heads):
    n = len(bufs)
    hd = HEAD_DIM
    tt = min(1024, seq)
    n_t = seq // tt
    return pl.pallas_call(
        functools.partial(_kv_rows_kernel, n=n),
        grid=(batch, n_heads, n_t),
        in_specs=[pl.BlockSpec((tt, hd), functools.partial(lambda b, h, i, c: (b * n_t + i, c + h), c=c))
                  for c in col_blocks] + [pl.BlockSpec(memory_space=pl.ANY)] * n,
        out_specs=[pl.BlockSpec((None, None, None, tt, hd), lambda b, h, i: (layer, b, h, i, 0))] * n,
        out_shape=[jax.ShapeDtypeStruct(buf.shape, buf.dtype) for buf in bufs],
        input_output_aliases={n + k: k for k in range(n)},
        compiler_params=_params("parallel", "parallel", "parallel"),
        name="store_kv_rows",
    )(*([proj] * n), *bufs)


def _dot3(x, w):
    xh, xl = _split(x)
    wh, wl = _split(w)
    m = x.shape[0]
    a = jnp.dot(jnp.concatenate([xh, xl], axis=0), wh, preferred_element_type=F32)
    return a[:m] + a[m:] + jnp.dot(xh, wl, preferred_element_type=F32)


def _dot3_nt(x, k):
    xh, xl = _split(x)
    kh, kl = _split(k)
    m = x.shape[0]
    a = _dot_nt(jnp.concatenate([xh, xl], axis=0), kh)
    return a[:m] + a[m:] + _dot_nt(xh, kl)


def _proj3_kernel(x_ref, w_ref, o_ref):
    o_ref[...] = _dot3(x_ref[...], w_ref[...])


def _proj3(x, w_stack, layer, tn):
    m, k = x.shape
    n = w_stack.shape[2]
    tn = min(tn, n)
    return pl.pallas_call(
        _proj3_kernel,
        grid=(n // tn,),
        in_specs=[pl.BlockSpec((m, k), lambda j: (0, 0)),
                  pl.BlockSpec((None, k, tn), lambda j: (layer, 0, j))],
        out_specs=pl.BlockSpec((m, tn), lambda j: (0, j)),
        out_shape=jax.ShapeDtypeStruct((m, n), F32),
        compiler_params=_params("parallel"),
        name="sample_proj",
    )(x, w_stack)


def _out_ln3_kernel(x_ref, w_ref, r_ref, g_ref, b_ref, o_ref, acc_ref, *, alpha):
    k = pl.program_id(0)

    @pl.when(k == 0)
    def _():
        acc_ref[...] = jnp.zeros_like(acc_ref)

    acc_ref[...] += _dot3(x_ref[...], w_ref[...])

    @pl.when(k == pl.num_programs(0) - 1)
    def _():
        o_ref[...] = _layer_norm(alpha * r_ref[...] + acc_ref[...], g_ref[...], b_ref[...])


def _out_ln3(x, w_stack, res, g, b, layer, alpha, tk):
    m, kdim = x.shape
    n = w_stack.shape[2]
    tk = min(tk, kdim)
    return pl.pallas_call(
        functools.partial(_out_ln3_kernel, alpha=alpha),
        grid=(kdim // tk,),
        in_specs=[pl.BlockSpec((m, tk), lambda k: (0, k)),
                  pl.BlockSpec((None, tk, n), lambda k: (layer, k, 0)),
                  pl.BlockSpec((m, n), lambda k: (0, 0)),
                  pl.BlockSpec((None, 1, n), lambda k: (layer, 0, 0)),
                  pl.BlockSpec((None, 1, n), lambda k: (layer, 0, 0))],
        out_specs=pl.BlockSpec((m, n), lambda k: (0, 0)),
        out_shape=jax.ShapeDtypeStruct((m, n), F32),
        scratch_shapes=[pltpu.VMEM((m, n), F32)],
        compiler_params=_params("arbitrary"),
        name="sample_out_ln",
    )(x, w_stack, res, g, b)


def _mlp3_kernel(x_ref, wu_ref, wd_ref, g_ref, b_ref, o_ref, acc_ref, *, alpha):
    f = pl.program_id(0)

    @pl.when(f == 0)
    def _():
        acc_ref[...] = jnp.zeros_like(acc_ref)

    h = jnp.maximum(_dot3(x_ref[...], wu_ref[...]), 0.0)
    acc_ref[...] += _dot3(h * h, wd_ref[...])

    @pl.when(f == pl.num_programs(0) - 1)
    def _():
        o_ref[...] = _layer_norm(alpha * x_ref[...] + acc_ref[...], g_ref[...], b_ref[...])


def _mlp3(x, wu_stack, wd_stack, g, b, layer, alpha, tf):
    m, d = x.shape
    ff = wu_stack.shape[2]
    tf = min(tf, ff)
    return pl.pallas_call(
        functools.partial(_mlp3_kernel, alpha=alpha),
        grid=(ff // tf,),
        in_specs=[pl.BlockSpec((m, d), lambda f: (0, 0)),
                  pl.BlockSpec((None, d, tf), lambda f: (layer, 0, f)),
                  pl.BlockSpec((None, tf, d), lambda f: (layer, f, 0)),
                  pl.BlockSpec((None, 1, d), lambda f: (layer, 0, 0)),
                  pl.BlockSpec((None, 1, d), lambda f: (layer, 0, 0))],
        out_specs=pl.BlockSpec((m, d), lambda f: (0, 0)),
        out_shape=jax.ShapeDtypeStruct((m, d), F32),
        scratch_shapes=[pltpu.VMEM((m, d), F32)],
        compiler_params=_params("arbitrary"),
        name="sample_mlp",
    )(x, wu_stack, wd_stack, g, b)


def _pick_rows(parts):
    row = lax.broadcasted_iota(jnp.int32, parts[0].shape, 0)
    out = jnp.zeros_like(parts[0])
    for h, part in enumerate(parts):
        out = jnp.where(row == h, part, out)
    return out


def _own_head_blocks(x, n_heads):
    return _pick_rows([x[:, h * HEAD_DIM:(h + 1) * HEAD_DIM] for h in range(n_heads)])


def _head_scores(q, keys):
    row = lax.broadcasted_iota(jnp.int32, q.shape, 0)
    q_bd = jnp.concatenate([jnp.where(row == h, q, 0.0) for h in range(len(keys))], axis=1)
    return _dot3_nt(q_bd, jnp.concatenate(keys, axis=1)) * ATTN_SCALE


def _head_values(w, values):
    return _dot3(w, jnp.concatenate(values, axis=1))


def _suffix_sums3(lk, uu):
    hi, lo = _split(lk)
    lo2 = (lk - hi.astype(F32) - lo.astype(F32)).astype(BF16)
    loc = jnp.dot(jnp.concatenate([hi, lo], axis=1), uu, preferred_element_type=F32)
    loc = loc + jnp.dot(lo2, uu[:SUFFIX_CHUNK], preferred_element_type=F32)
    return loc, loc[:, 0:1] + lk[:, 0:1]


def _sb_decode_kernel(pt_ref, q_ref, *refs, n_heads, n_sub):
    k_refs, v_refs = refs[:n_sub], refs[n_sub:2 * n_sub]
    uu_ref, g_ref, o_ref, acc_ref, carry_ref = refs[2 * n_sub:]
    j = pl.program_id(1)

    @pl.when(j == 0)
    def _():
        acc_ref[...] = jnp.zeros_like(acc_ref)
        carry_ref[...] = jnp.zeros_like(carry_ref)

    def tokens(page_refs, h):
        return jnp.concatenate([r[h] for r in page_refs], axis=0)

    z = _head_scores(q_ref[...], [tokens(k_refs, h) for h in range(n_heads)])
    nsp = _neg_softplus(z)
    uu = uu_ref[...]
    run = carry_ref[...]
    n_chunks = z.shape[1] // SUFFIX_CHUNK
    between = [None] * n_chunks
    for c in reversed(range(n_chunks)):
        loc, tot = _suffix_sums3(nsp[:, c * SUFFIX_CHUNK:(c + 1) * SUFFIX_CHUNK], uu)
        between[c] = loc + run
        run = run + tot
    w = jnp.exp(z + nsp + jnp.concatenate(between, axis=1))
    acc_ref[...] += _head_values(w, [tokens(v_refs, h) for h in range(n_heads)])
    carry_ref[...] = run

    @pl.when(j == pl.num_programs(1) - 1)
    def _():
        o_ref[...] = _head_rms(_own_head_blocks(acc_ref[...], n_heads), g_ref[...])


def _moba_gate_kernel(pt_ref, q_ref, *refs, n_top, n_sub, pages_per_block):
    k_refs = refs[:n_sub * pages_per_block]
    idx_ref, gate_ref = refs[n_sub * pages_per_block:]
    n = pl.program_id(1)

    @pl.when(n == 0)
    def _():
        gate_ref[...] = jnp.full_like(gate_ref, NEG_INF)

    n_heads = k_refs[0].shape[0]
    lane = lax.broadcasted_iota(jnp.int32, (n_heads, HEAD_DIM), 1)
    gate = gate_ref[0:n_heads, :]
    for b in range(n_sub):
        pages = k_refs[b * pages_per_block:(b + 1) * pages_per_block]
        mean = sum(jnp.sum(r[...], axis=1) for r in pages) * (1.0 / MOBA_BLOCK)
        g = jnp.sum(q_ref[0:n_heads, :] * mean, axis=-1, keepdims=True)
        gate = jnp.where(lane == n * n_sub + b, g, gate)
    gate_ref[0:n_heads, :] = gate

    @pl.when(n == pl.num_programs(1) - 1)
    def _():
        gate = gate_ref[...]
        lanes = lax.broadcasted_iota(jnp.int32, gate.shape, 1)
        idx = jnp.zeros(gate.shape, jnp.int32)
        for r in range(n_top):
            best = jnp.max(gate, axis=-1, keepdims=True)
            first = jnp.min(jnp.where(gate == best, lanes, HEAD_DIM), axis=-1, keepdims=True)
            idx = jnp.where(lanes == r, first, idx)
            gate = jnp.where(lanes == first, NEG_INF, gate)
        idx_ref[...] = idx


def _moba_decode_kernel(pt_ref, top_ref, slope_ref, q_ref, kn_ref, vn_ref, g_ref, *refs, n_heads, past_len, page):
    k_refs, v_refs = refs[:n_heads], refs[n_heads:2 * n_heads]
    o_ref, acc_ref, m_ref, l_ref = refs[2 * n_heads:]
    s_id, t = pl.program_id(0), pl.program_id(1)
    pages_per_block = MOBA_BLOCK // page
    q = q_ref[...]

    @pl.when(t == 0)
    def _():
        m_ref[...] = jnp.sum(q * kn_ref[...], axis=-1, keepdims=True) * ATTN_SCALE
        l_ref[...] = jnp.ones_like(l_ref)
        acc_ref[...] = vn_ref[...]

    row = lax.broadcasted_iota(jnp.int32, (8, 1), 0)
    slope = jnp.zeros((8, 1), F32)
    dist0 = jnp.zeros((8, 1), F32)
    for h in range(n_heads):
        kpos0 = top_ref[s_id, h, t // pages_per_block] * MOBA_BLOCK + (t % pages_per_block) * page
        slope = jnp.where(row == h, slope_ref[h], slope)
        dist0 = jnp.where(row == h, (past_len - kpos0).astype(F32), dist0)
    lane = lax.broadcasted_iota(jnp.int32, (8, page), 1)
    s = _head_scores(q, [r[...] for r in k_refs]) - slope * (dist0 - lane.astype(F32))
    m_old = m_ref[...]
    m_new = jnp.maximum(m_old, jnp.max(s, axis=-1, keepdims=True))
    a = jnp.exp(m_old - m_new)
    e = jnp.exp(s - m_new)
    l_ref[...] = a * l_ref[...] + jnp.sum(e, axis=-1, keepdims=True)
    acc_ref[...] = a * acc_ref[...] + _own_head_blocks(_head_values(e, [r[...] for r in v_refs]), n_heads)
    m_ref[...] = m_new

    @pl.when(t == pl.num_programs(1) - 1)
    def _():
        o_ref[...] = _head_rms(acc_ref[...] / l_ref[...], g_ref[...])


def _mem_decode_kernel(q_ref, k_ref, v_ref, g_ref, o_ref, *, n_heads):
    s = _head_scores(q_ref[...], [k_ref[:, h, :] for h in range(n_heads)])
    e = jnp.exp(s - jnp.max(s, axis=-1, keepdims=True))
    o = _head_values(e, [v_ref[:, h, :] for h in range(n_heads)])
    o = _own_head_blocks(o, n_heads) / jnp.sum(e, axis=-1, keepdims=True)
    o_ref[...] = _head_rms(o, g_ref[...])


def _heads_to_rows(x, n_heads):
    x = x.reshape(x.shape[0], n_heads, HEAD_DIM)
    return jnp.pad(x, ((0, 0), (0, 8 - n_heads), (0, 0)))


def _pages_per_step(n, most):
    return max(d for d in range(1, most + 1) if n % d == 0)


def _sample_attention(proj, gain_l, page_table, pools, mem_k, mem_v, slopes, uu, layer, h_sb, h_mb, h_mem):
    cache_k_sb, cache_v_sb, cache_k_mb, cache_v_mb = pools
    n_seq, n_pages = page_table.shape
    page = cache_k_sb.shape[3]
    past_len = n_pages * page
    hd = HEAD_DIM
    w_sb, w_mb = h_sb * hd, h_mb * hd
    q_sb = _heads_to_rows(proj[:, 0:w_sb], h_sb)
    q_mb = _heads_to_rows(proj[:, 3 * w_sb:3 * w_sb + w_mb], h_mb)
    k_mb_new = _heads_to_rows(proj[:, 3 * w_sb + w_mb:3 * w_sb + 2 * w_mb], h_mb)
    v_mb_new = _heads_to_rows(proj[:, 3 * w_sb + 2 * w_mb:3 * w_sb + 3 * w_mb], h_mb)
    q_mem = _heads_to_rows(proj[:, 3 * w_sb + 3 * w_mb:], h_mem)
    g_sb = _heads_to_rows(gain_l[None, 0:w_sb], h_sb)[0]
    g_mb = _heads_to_rows(gain_l[None, w_sb:w_sb + w_mb], h_mb)[0]
    g_mem = _heads_to_rows(gain_l[None, w_sb + w_mb:], h_mem)[0]

    row_spec = pl.BlockSpec((None, 8, hd), lambda s, j, *_: (s, 0, 0))
    gain_spec = pl.BlockSpec((8, hd), lambda s, j, *_: (0, 0))
    row_shape = jax.ShapeDtypeStruct((n_seq, 8, hd), F32)

    def page_spec(n_heads, page_of):
        return pl.BlockSpec((None, None, n_heads, page, hd),
                            lambda s, j, pt: (layer, pt[s, page_of(j)], 0, 0, 0))

    n_sub = _pages_per_step(n_pages, 8)
    sb_pages = [page_spec(h_sb, functools.partial(lambda j, i: n_pages - n_sub * (j + 1) + i, i=i))
                for i in range(n_sub)]
    o_sb = pl.pallas_call(
        functools.partial(_sb_decode_kernel, n_heads=h_sb, n_sub=n_sub),
        grid_spec=pltpu.PrefetchScalarGridSpec(
            num_scalar_prefetch=1, grid=(n_seq, n_pages // n_sub),
            in_specs=[row_spec] + sb_pages + sb_pages
                     + [pl.BlockSpec((2 * SUFFIX_CHUNK, SUFFIX_CHUNK), lambda s, j, pt: (0, 0)), gain_spec],
            out_specs=row_spec,
            scratch_shapes=[pltpu.VMEM((8, h_sb * hd), F32), pltpu.VMEM((8, 1), F32)]),
        out_shape=row_shape,
        compiler_params=_params("parallel", "arbitrary"),
        name="sample_sb_attn",
    )(page_table, q_sb, *([cache_k_sb] * n_sub), *([cache_v_sb] * n_sub), uu, g_sb)

    pages_per_block = MOBA_BLOCK // page
    n_past_blocks = past_len // MOBA_BLOCK
    n_top = min(MOBA_TOPK, n_past_blocks)
    n_gate = _pages_per_step(n_past_blocks, 8)
    gate_pages = [page_spec(h_mb, functools.partial(lambda n, i: n_gate * pages_per_block * n + i, i=i))
                  for i in range(n_gate * pages_per_block)]
    top, _ = pl.pallas_call(
        functools.partial(_moba_gate_kernel, n_top=n_top, n_sub=n_gate, pages_per_block=pages_per_block),
        grid_spec=pltpu.PrefetchScalarGridSpec(
            num_scalar_prefetch=1, grid=(n_seq, n_past_blocks // n_gate),
            in_specs=[row_spec] + gate_pages,
            out_specs=[row_spec, row_spec]),
        out_shape=[jax.ShapeDtypeStruct((n_seq, 8, hd), jnp.int32), row_shape],
        compiler_params=_params("parallel", "arbitrary"),
        name="sample_moba_gate",
    )(page_table, q_mb, *([cache_k_mb] * (n_gate * pages_per_block)))
    top = top[:, :, :max(n_top, 1)]

    def sel_page(h):
        return pl.BlockSpec(
            (None, None, None, page, hd),
            lambda s, t, pt, tp: (layer, pt[s, tp[s, h, t // pages_per_block] * pages_per_block + t % pages_per_block],
                                  h, 0, 0))

    sel_pages = [sel_page(h) for h in range(h_mb)]
    o_mb = pl.pallas_call(
        functools.partial(_moba_decode_kernel, n_heads=h_mb, past_len=past_len, page=page),
        grid_spec=pltpu.PrefetchScalarGridSpec(
            num_scalar_prefetch=2, grid=(n_seq, n_top * pages_per_block),
            in_specs=[pl.BlockSpec(memory_space=pltpu.SMEM), row_spec, row_spec, row_spec, gain_spec]
                     + sel_pages + sel_pages,
            out_specs=row_spec,
            scratch_shapes=[pltpu.VMEM((8, hd), F32), pltpu.VMEM((8, 1), F32), pltpu.VMEM((8, 1), F32)]),
        out_shape=row_shape,
        compiler_params=_params("parallel", "arbitrary"),
        name="sample_moba_attn",
    )(page_table, top, slopes, q_mb, k_mb_new, v_mb_new, g_mb, *([cache_k_mb] * h_mb), *([cache_v_mb] * h_mb))

    mem_len = mem_k.shape[2]
    mem_spec = pl.BlockSpec((None, None, mem_len, h_mem, hd), lambda s: (layer, s, 0, 0, 0))
    o_mem = pl.pallas_call(
        functools.partial(_mem_decode_kernel, n_heads=h_mem),
        grid=(n_seq,),
        in_specs=[pl.BlockSpec((None, 8, hd), lambda s: (s, 0, 0)), mem_spec, mem_spec,
                  pl.BlockSpec((8, hd), lambda s: (0, 0))],
        out_specs=pl.BlockSpec((None, 8, hd), lambda s: (s, 0, 0)),
        out_shape=row_shape,
        compiler_params=_params("parallel"),
        name="sample_mem_attn",
    )(q_mem, mem_k, mem_v, g_mem)

    return jnp.concatenate([o_sb[:, :h_sb].reshape(n_seq, w_sb), o_mb[:, :h_mb].reshape(n_seq, w_mb),
                            o_mem[:, :h_mem].reshape(n_seq, h_mem * hd)], axis=1)


def kernel(x_prompt, x_sample, mem_prompt, cache_k_sb, cache_v_sb, cache_k_moba, cache_v_moba, cache_mem_k, cache_mem_v, page_table, w_in, w_mem_kv, out_norm_g, w_o, ln1_g, ln1_b, w_up, w_down, ln2_g, ln2_b):
    batch, seq, d_model = x_prompt.shape
    n_seq, dec_seq, _ = x_sample.shape
    depth = w_in.shape[0]
    h_sb, h_mb, h_mem = cache_k_sb.shape[3], cache_k_moba.shape[3], cache_mem_k.shape[3]
    mem_len = mem_prompt.shape[1]
    hd = HEAD_DIM
    assert dec_seq == 1 and cache_k_sb.shape[4] == hd and seq % MOBA_KEYS == 0
    assert MOBA_BLOCK % cache_k_sb.shape[2] == 0 and seq // MOBA_BLOCK <= hd
    w_mem = h_mem * hd
    alpha = (2 * depth) ** 0.25

    w_in_b, w_kv_b, w_o_b = w_in.astype(BF16), w_mem_kv.astype(BF16), w_o.astype(BF16)
    w_up_b, w_down_b = w_up.astype(BF16), w_down.astype(BF16)
    gain = out_norm_g.reshape(depth, 1, d_model)
    ln1_g3, ln1_b3 = ln1_g.reshape(depth, 1, d_model), ln1_b.reshape(depth, 1, d_model)
    ln2_g3, ln2_b3 = ln2_g.reshape(depth, 1, d_model), ln2_b.reshape(depth, 1, d_model)
    slopes = jnp.exp2(-8.0 * jnp.arange(1, h_mb + 1, dtype=F32) / h_mb)
    uu = _suffix_matrix()
    pools = tuple(jnp.transpose(c, (0, 1, 3, 2, 4)) for c in (cache_k_sb, cache_v_sb, cache_k_moba, cache_v_moba))

    hp = x_prompt.reshape(batch * seq, d_model)
    hs = x_sample.reshape(n_seq, d_model)
    mem2d = mem_prompt.reshape(batch * mem_len, d_model)
    assert h_sb == h_mb
    kv_p = [jnp.zeros((depth, batch, h_sb, seq, hd), F32) for _ in range(4)]
    kv_cols = (h_sb, 2 * h_sb, 3 * h_sb + h_mb, 3 * h_sb + 2 * h_mb)
    outs = [[] for _ in range(10)]
    for l in range(depth):
        memkv = _proj(mem2d, w_kv_b, l, 512, 512)
        proj_p = _proj(hp, w_in_b, l, 1024, 1024)
        mix_p = _prompt_attention(proj_p, memkv, gain, slopes, uu, l, batch, seq, h_sb, h_mb, h_mem, mem_len)
        hp = _out_ln(mix_p, w_o_b, hp, ln1_g3, ln1_b3, l, alpha, 1024, 512)
        hp = _mlp(hp, w_up_b, w_down_b, ln2_g3, ln2_b3, l, alpha, 512, 1024)

        proj_s = _proj3(hs, w_in, l, 512)
        mix_s = _sample_attention(proj_s, out_norm_g[l], page_table, pools, cache_mem_k, cache_mem_v,
                                  slopes, uu, l, h_sb, h_mb, h_mem)
        hs = _out_ln3(mix_s, w_o, hs, ln1_g3, ln1_b3, l, alpha, 512)
        hs = _mlp3(hs, w_up, w_down, ln2_g3, ln2_b3, l, alpha, 512)

        kv_p = _store_kv_rows(proj_p, kv_p, kv_cols, l, batch, seq, h_sb)
        for i, c in enumerate(kv_cols):
            outs[6 + i].append(proj_s[:, c * hd:(c + h_sb) * hd].reshape(n_seq, 1, h_sb, hd))
        outs[4].append(memkv[:, :w_mem].reshape(batch, mem_len, h_mem, hd))
        outs[5].append(memkv[:, w_mem:].reshape(batch, mem_len, h_mem, hd))
    return (hp.reshape(batch, seq, d_model), hs.reshape(n_seq, 1, d_model),
            *[jnp.transpose(buf, (0, 1, 3, 2, 4)) for buf in kv_p],
            *[jnp.stack(o) for o in outs[4:]])
```

```python
import functools

import numpy as np
import jax
import jax.numpy as jnp
from jax import lax
from jax.experimental import pallas as pl
from jax.experimental.pallas import tpu as pltpu

F32 = jnp.float32
BF16 = jnp.bfloat16

HEAD_DIM = 128
MOBA_BLOCK = 256
MOBA_TOPK = 3
LN_EPS = 1e-5
RMS_EPS = 1e-6
NEG_INF = -1e30
ATTN_SCALE = HEAD_DIM ** -0.5
SB_TILE = 512
SB_HEADS_PER_STEP = 2
MOBA_KEYS = 4 * MOBA_BLOCK
LOG2E = 1.4426950408889634
SUFFIX_CHUNK = 128
VMEM_LIMIT = 56 * 1024 * 1024


def _params(*sem):
    return pltpu.CompilerParams(dimension_semantics=sem, vmem_limit_bytes=VMEM_LIMIT)


def _dot_nt(a, b):
    return lax.dot_general(a, b, (((1,), (1,)), ((), ())), preferred_element_type=F32)


def _head_rms(o, g):
    return o * lax.rsqrt(jnp.mean(o * o, axis=-1, keepdims=True) + RMS_EPS) * g


def _layer_norm(y, g, b):
    mu = jnp.mean(y, axis=-1, keepdims=True)
    d = y - mu
    var = jnp.mean(d * d, axis=-1, keepdims=True)
    return d * lax.rsqrt(var + LN_EPS) * g + b


def _neg_softplus(z):
    return -(jnp.maximum(z, 0.0) + jnp.log(1.0 + jnp.exp(-jnp.abs(z))))


def _split(x):
    hi = x.astype(BF16)
    return hi, (x - hi.astype(F32)).astype(BF16)


def _suffix_matrix():
    j = np.arange(SUFFIX_CHUNK)
    u = (j[:, None] > j[None, :]).astype(np.float32)
    return jnp.asarray(np.concatenate([u, u], axis=0), dtype=BF16)


def _proj_kernel(x_ref, w_ref, o_ref, xb_ref):
    @pl.when(pl.program_id(1) == 0)
    def _():
        xb_ref[...] = x_ref[...].astype(BF16)

    o_ref[...] = jnp.dot(xb_ref[...], w_ref[...], preferred_element_type=F32)


def _proj(x, w_stack, layer, tm, tn):
    m, k = x.shape
    n = w_stack.shape[2]
    tm, tn = min(tm, m), min(tn, n)
    return pl.pallas_call(
        _proj_kernel,
        grid=(m // tm, n // tn),
        in_specs=[pl.BlockSpec((tm, k), lambda i, j: (i, 0)),
                  pl.BlockSpec((None, k, tn), lambda i, j: (layer, 0, j))],
        out_specs=pl.BlockSpec((tm, tn), lambda i, j: (i, j)),
        out_shape=jax.ShapeDtypeStruct((m, n), F32),
        scratch_shapes=[pltpu.VMEM((tm, k), BF16)],
        compiler_params=_params("parallel", "arbitrary"),
        name="proj",
    )(x, w_stack)


def _out_ln_kernel(x_ref, w_ref, r_ref, g_ref, b_ref, o_ref, acc_ref, *, alpha):
    k = pl.program_id(1)

    @pl.when(k == 0)
    def _():
        acc_ref[...] = jnp.zeros_like(acc_ref)

    acc_ref[...] += jnp.dot(x_ref[...], w_ref[...], preferred_element_type=F32)

    @pl.when(k == pl.num_programs(1) - 1)
    def _():
        o_ref[...] = _layer_norm(alpha * r_ref[...] + acc_ref[...], g_ref[...], b_ref[...])


def _out_ln(x, w_stack, res, g, b, layer, alpha, tm, tk):
    m, kdim = x.shape
    n = w_stack.shape[2]
    tm, tk = min(tm, m), min(tk, kdim)
    return pl.pallas_call(
        functools.partial(_out_ln_kernel, alpha=alpha),
        grid=(m // tm, kdim // tk),
        in_specs=[pl.BlockSpec((tm, tk), lambda i, k: (i, k)),
                  pl.BlockSpec((None, tk, n), lambda i, k: (layer, k, 0)),
                  pl.BlockSpec((tm, n), lambda i, k: (i, 0)),
                  pl.BlockSpec((None, 1, n), lambda i, k: (layer, 0, 0)),
                  pl.BlockSpec((None, 1, n), lambda i, k: (layer, 0, 0))],
        out_specs=pl.BlockSpec((tm, n), lambda i, k: (i, 0)),
        out_shape=jax.ShapeDtypeStruct((m, n), F32),
        scratch_shapes=[pltpu.VMEM((tm, n), F32)],
        compiler_params=_params("parallel", "arbitrary"),
        name="out_ln",
    )(x, w_stack, res, g, b)


def _mlp_kernel(x_ref, wu_ref, wd_ref, g_ref, b_ref, o_ref, acc_ref, xb_ref, *, alpha):
    f = pl.program_id(1)

    @pl.when(f == 0)
    def _():
        xb_ref[...] = x_ref[...].astype(BF16)
        acc_ref[...] = jnp.zeros_like(acc_ref)

    h = jnp.maximum(jnp.dot(xb_ref[...], wu_ref[...], preferred_element_type=F32), 0.0)
    acc_ref[...] += jnp.dot((h * h).astype(BF16), wd_ref[...], preferred_element_type=F32)

    @pl.when(f == pl.num_programs(1) - 1)
    def _():
        o_ref[...] = _layer_norm(alpha * x_ref[...] + acc_ref[...], g_ref[...], b_ref[...])


def _mlp(x, wu_stack, wd_stack, g, b, layer, alpha, tm, tf):
    m, d = x.shape
    ff = wu_stack.shape[2]
    tm, tf = min(tm, m), min(tf, ff)
    return pl.pallas_call(
        functools.partial(_mlp_kernel, alpha=alpha),
        grid=(m // tm, ff // tf),
        in_specs=[pl.BlockSpec((tm, d), lambda i, f: (i, 0)),
                  pl.BlockSpec((None, d, tf), lambda i, f: (layer, 0, f)),
                  pl.BlockSpec((None, tf, d), lambda i, f: (layer, f, 0)),
                  pl.BlockSpec((None, 1, d), lambda i, f: (layer, 0, 0)),
                  pl.BlockSpec((None, 1, d), lambda i, f: (layer, 0, 0))],
        out_specs=pl.BlockSpec((tm, d), lambda i, f: (i, 0)),
        out_shape=jax.ShapeDtypeStruct((m, d), F32),
        scratch_shapes=[pltpu.VMEM((tm, d), F32), pltpu.VMEM((tm, d), BF16)],
        compiler_params=_params("parallel", "arbitrary"),
        name="mlp",
    )(x, wu_stack, wd_stack, g, b)


def _sb_prompt_kernel(q_ref, k_ref, v_ref, uu_ref, g_ref, o_ref, acc_ref, carry_ref, qb_ref):
    qi = pl.program_id(2)
    t = q_ref.shape[0]
    heads = [slice(h * HEAD_DIM, (h + 1) * HEAD_DIM) for h in range(q_ref.shape[1] // HEAD_DIM)]
    chunks = [slice(c * SUFFIX_CHUNK, (c + 1) * SUFFIX_CHUNK) for c in range(t // SUFFIX_CHUNK)]
    acc_ref[...] = jnp.zeros_like(acc_ref)
    carry_ref[...] = jnp.zeros_like(carry_ref)
    qb_ref[...] = q_ref[...].astype(BF16)

    def tile(kj, diagonal):
        for h, cols in enumerate(heads):
            head_tile(kj, diagonal, h, cols)

    def head_tile(kj, diagonal, h, cols):
        rows = pl.ds(pl.multiple_of(kj * t, t), t)
        z = _dot_nt(qb_ref[:, cols], k_ref[rows, cols].astype(BF16)) * ATTN_SCALE
        nsp = _neg_softplus(z)
        if diagonal:
            past = lax.broadcasted_iota(jnp.int32, (t, t), 1) < lax.broadcasted_iota(jnp.int32, (t, t), 0)
            lk = jnp.where(past, nsp, 0.0)
        else:
            lk = nsp
        hi, lo = _split(lk)
        uu = uu_ref[...]
        locs = [jnp.dot(jnp.concatenate([hi[:, c], lo[:, c]], axis=1), uu, preferred_element_type=F32)
                for c in chunks]
        run = carry_ref[h]
        between = [None] * len(chunks)
        for i in reversed(range(len(chunks))):
            between[i] = locs[i] + run
            run = run + (locs[i][:, 0:1] + lk[:, chunks[i].start:chunks[i].start + 1])
        w = jnp.exp(z + nsp + jnp.concatenate(between, axis=1))
        if diagonal:
            w = jnp.where(past, w, 0.0)
        acc_ref[:, cols] += jnp.dot(w.astype(BF16), v_ref[rows, cols].astype(BF16), preferred_element_type=F32)
        carry_ref[h] = run

    tile(qi, True)

    def older(i, carry):
        tile(qi - 1 - i, False)
        return carry

    lax.fori_loop(0, qi, older, 0)
    for cols in heads:
        o_ref[:, cols] = _head_rms(acc_ref[:, cols], g_ref[:, cols]).astype(o_ref.dtype)


def _moba_prompt_kernel(slope_ref, q_ref, k_ref, v_ref, km_ref, g_ref, o_ref,
                        acc_ref, m_ref, l_ref, qb_ref, sel_ref, bias_ref, *, n_top, n_blk, blocks_per_step):
    qi = pl.program_id(2)
    t = MOBA_BLOCK
    own_tile = qi // blocks_per_step
    own_slot = qi % blocks_per_step
    n_heads = q_ref.shape[1] // HEAD_DIM
    heads = [slice(h * HEAD_DIM, (h + 1) * HEAD_DIM) for h in range(n_heads)]
    slopes2 = [slope_ref[pl.program_id(1) * n_heads + h] * LOG2E for h in range(n_heads)]

    acc_ref[...] = jnp.zeros_like(acc_ref)
    m_ref[...] = jnp.full_like(m_ref, NEG_INF)
    l_ref[...] = jnp.zeros_like(l_ref)
    qb_ref[...] = (q_ref[...] * (ATTN_SCALE * LOG2E)).astype(BF16)
    blk_row = lax.broadcasted_iota(jnp.int32, (HEAD_DIM, t), 0)
    rel = (lax.broadcasted_iota(jnp.int32, (t, t), 1) - lax.broadcasted_iota(jnp.int32, (t, t), 0)).astype(F32)
    for h, cols in enumerate(heads):
        gate = lax.dot_general(km_ref[h], q_ref[:, cols], (((1,), (1,)), ((), ())),
                               preferred_element_type=F32, precision=lax.Precision.HIGHEST)
        sel_t = jnp.zeros((HEAD_DIM, t), F32)
        for n in range(n_blk):
            g_n = gate[n:n + 1, :]
            beats = ((gate > g_n) | ((gate == g_n) & (blk_row < n))) & (blk_row < qi)
            rank = jnp.sum(jnp.where(beats, 1.0, 0.0), axis=0, keepdims=True)
            sel_t = jnp.where(blk_row == n, jnp.where(rank < n_top, 1.0, 0.0), sel_t)
        sel_ref[h] = sel_t.T
        bias_ref[h] = rel * slopes2[h]

    def selected(h, blk):
        sel = sel_ref[h]
        lane = lax.broadcasted_iota(jnp.int32, sel.shape, 1)
        return jnp.sum(jnp.where(lane == blk, sel, 0.0), axis=-1, keepdims=True) > 0.5

    def update(kj, slots):
        for h, cols in enumerate(heads):
            head_update(kj, slots, h, cols)

    def head_update(kj, slots, h, cols):
        rows = [pl.ds(pl.multiple_of(kj * (blocks_per_step * t) + s * t, t), t) for s, _, _ in slots]
        z = _dot_nt(qb_ref[:, cols], jnp.concatenate([k_ref[r, cols] for r in rows], axis=0).astype(BF16))
        bias = bias_ref[h]
        parts = []
        for i, (_, blk, kind) in enumerate(slots):
            s = (z[:, i * t:(i + 1) * t] - slopes2[h] * ((qi - blk) * t).astype(F32)) + bias
            if kind == "own":
                mask = lax.broadcasted_iota(jnp.int32, (t, t), 1) <= lax.broadcasted_iota(jnp.int32, (t, t), 0)
            else:
                mask = selected(h, blk)
            parts.append(jnp.where(mask, s, NEG_INF))
        m_old = m_ref[h]
        m_new = m_old
        for s in parts:
            m_new = jnp.maximum(m_new, jnp.max(s, axis=-1, keepdims=True))
        a = jnp.exp2(m_old - m_new)
        es = [jnp.exp2(s - m_new) for s in parts]
        l_new = a * l_ref[h]
        for e in es:
            l_new = l_new + jnp.sum(e, axis=-1, keepdims=True)
        e_all = jnp.concatenate(es, axis=1).astype(BF16)
        v_all = jnp.concatenate([v_ref[r, cols] for r in rows], axis=0).astype(BF16)
        acc_ref[:, cols] = a * acc_ref[:, cols] + jnp.dot(e_all, v_all, preferred_element_type=F32)
        l_ref[h] = l_new
        m_ref[h] = m_new

    for r in range(blocks_per_step):
        @pl.when(own_slot == r)
        def _(r=r):
            update(own_tile, [(r, qi, "own")] + [(i, qi - r + i, "past") for i in range(r)])

    def older(kj, carry):
        update(kj, [(i, blocks_per_step * kj + i, "past") for i in range(blocks_per_step)])
        return carry

    lax.fori_loop(0, own_tile, older, 0)
    for h, cols in enumerate(heads):
        o_ref[:, cols] = _head_rms(acc_ref[:, cols] / l_ref[h], g_ref[:, cols]).astype(o_ref.dtype)


def _mem_prompt_kernel(q_ref, k_ref, v_ref, g_ref, o_ref):
    s = _dot_nt(q_ref[...].astype(BF16), k_ref[...].astype(BF16)) * ATTN_SCALE
    e = jnp.exp(s - jnp.max(s, axis=-1, keepdims=True))
    o = jnp.dot(e.astype(BF16), v_ref[...].astype(BF16), preferred_element_type=F32)
    o = o / jnp.sum(e, axis=-1, keepdims=True)
    o_ref[...] = _head_rms(o, g_ref[...]).astype(o_ref.dtype)


def _block_mean_kernel(k_ref, o_ref, *, group):
    x = k_ref[...]
    o_ref[...] = jnp.sum(x.reshape(group, MOBA_BLOCK, x.shape[-1]), axis=1) * (1.0 / MOBA_BLOCK)


def _prompt_attention(proj, memkv, gain, slopes, uu, layer, batch, seq, h_sb, h_mb, h_mem, mem_len):
    rows = batch * seq
    hd = HEAD_DIM
    c_qsb, c_ksb, c_vsb = 0, h_sb, 2 * h_sb
    c_qmb, c_kmb, c_vmb = 3 * h_sb, 3 * h_sb + h_mb, 3 * h_sb + 2 * h_mb
    c_qmem = 3 * h_sb + 3 * h_mb
    assert c_kmb % h_mb == 0

    t = min(SB_TILE, seq)
    n_t = seq // t
    hps = SB_HEADS_PER_STEP
    assert h_sb % hps == 0 and c_ksb % hps == 0 and c_vsb % hps == 0
    wd = hps * hd
    o_sb = pl.pallas_call(
        _sb_prompt_kernel,
        grid=(batch, h_sb // hps, n_t),
        in_specs=[pl.BlockSpec((t, wd), lambda b, h, i: (b * n_t + i, c_qsb // hps + h)),
                  pl.BlockSpec((seq, wd), lambda b, h, i: (b, c_ksb // hps + h)),
                  pl.BlockSpec((seq, wd), lambda b, h, i: (b, c_vsb // hps + h)),
                  pl.BlockSpec((2 * SUFFIX_CHUNK, SUFFIX_CHUNK), lambda b, h, i: (0, 0)),
                  pl.BlockSpec((None, 1, wd), lambda b, h, i: (layer, 0, h))],
        out_specs=pl.BlockSpec((t, wd), lambda b, h, i: (b * n_t + i, h)),
        out_shape=jax.ShapeDtypeStruct((rows, h_sb * hd), BF16),
        scratch_shapes=[pltpu.VMEM((t, wd), F32), pltpu.VMEM((hps, t, 1), F32), pltpu.VMEM((t, wd), BF16)],
        compiler_params=_params("parallel", "parallel", "arbitrary"),
        name="prompt_sb_attn",
    )(proj, proj, proj, uu, gain)

    nblk = seq // MOBA_BLOCK
    group = min(8, batch * nblk)
    kmean = pl.pallas_call(
        functools.partial(_block_mean_kernel, group=group),
        grid=(batch * nblk // group,),
        in_specs=[pl.BlockSpec((group * MOBA_BLOCK, h_mb * hd), lambda i: (i, c_kmb // h_mb))],
        out_specs=pl.BlockSpec((group, h_mb * hd), lambda i: (i, 0)),
        out_shape=jax.ShapeDtypeStruct((batch * nblk, h_mb * hd), F32),
        compiler_params=_params("parallel"),
        name="moba_block_mean",
    )(proj)
    kmean = kmean.reshape(batch, nblk, h_mb, hd).transpose(0, 2, 1, 3)
    kmean = jnp.pad(kmean, ((0, 0), (0, 0), (0, hd - nblk), (0, 0)))

    blocks_per_step = MOBA_KEYS // MOBA_BLOCK
    assert h_mb % hps == 0 and c_qmb % hps == 0 and c_kmb % hps == 0 and c_vmb % hps == 0 and h_sb % hps == 0
    o_mb = pl.pallas_call(
        functools.partial(_moba_prompt_kernel, n_top=min(MOBA_TOPK, nblk), n_blk=nblk, blocks_per_step=blocks_per_step),
        grid=(batch, h_mb // hps, nblk),
        in_specs=[pl.BlockSpec(memory_space=pltpu.SMEM),
                  pl.BlockSpec((MOBA_BLOCK, wd), lambda b, h, i: (b * nblk + i, c_qmb // hps + h)),
                  pl.BlockSpec((seq, wd), lambda b, h, i: (b, c_kmb // hps + h)),
                  pl.BlockSpec((seq, wd), lambda b, h, i: (b, c_vmb // hps + h)),
                  pl.BlockSpec((None, hps, hd, hd), lambda b, h, i: (b, h, 0, 0)),
                  pl.BlockSpec((None, 1, wd), lambda b, h, i: (layer, 0, h_sb // hps + h))],
        out_specs=pl.BlockSpec((MOBA_BLOCK, wd), lambda b, h, i: (b * nblk + i, h)),
        out_shape=jax.ShapeDtypeStruct((rows, h_mb * hd), BF16),
        scratch_shapes=[pltpu.VMEM((MOBA_BLOCK, wd), F32), pltpu.VMEM((hps, MOBA_BLOCK, 1), F32),
                        pltpu.VMEM((hps, MOBA_BLOCK, 1), F32), pltpu.VMEM((MOBA_BLOCK, wd), BF16),
                        pltpu.VMEM((hps, MOBA_BLOCK, hd), F32), pltpu.VMEM((hps, MOBA_BLOCK, MOBA_BLOCK), F32)],
        compiler_params=_params("parallel", "parallel", "arbitrary"),
        name="prompt_moba_attn",
    )(slopes, proj, proj, proj, kmean, gain)

    tq = min(512, seq)
    nq = seq // tq
    o_mem = pl.pallas_call(
        _mem_prompt_kernel,
        grid=(batch, h_mem, nq),
        in_specs=[pl.BlockSpec((tq, hd), lambda b, h, i: (b * nq + i, c_qmem + h)),
                  pl.BlockSpec((mem_len, hd), lambda b, h, i: (b, h)),
                  pl.BlockSpec((mem_len, hd), lambda b, h, i: (b, h_mem + h)),
                  pl.BlockSpec((None, 1, hd), lambda b, h, i: (layer, 0, h_sb + h_mb + h))],
        out_specs=pl.BlockSpec((tq, hd), lambda b, h, i: (b * nq + i, h)),
        out_shape=jax.ShapeDtypeStruct((rows, h_mem * hd), BF16),
        compiler_params=_params("parallel", "parallel", "parallel"),
        name="prompt_mem_attn",
    )(proj, memkv, memkv, gain)
    return jnp.concatenate([o_sb, o_mb, o_mem], axis=1)


def _kv_rows_kernel(*refs, n):
    for src_ref, dst_ref in zip(refs[:n], refs[2 * n:]):
        dst_ref[...] = src_ref[...]


def _store_kv_rows(proj, bufs, col_blocks, layer, batch, seq, n_heads):
    n = len(bufs)
    hd = HEAD_DIM
    tt = min(1024, seq)
    n_t = seq // tt
    return pl.pallas_call(
        functools.partial(_kv_rows_kernel, n=n),
        grid=(batch, n_heads, n_t),
        in_specs=[pl.BlockSpec((tt, hd), functools.partial(lambda b, h, i, c: (b * n_t + i, c + h), c=c))
                  for c in col_blocks] + [pl.BlockSpec(memory_space=pl.ANY)] * n,
        out_specs=[pl.BlockSpec((None, None, None, tt, hd), lambda b, h, i: (layer, b, h, i, 0))] * n,
        out_shape=[jax.ShapeDtypeStruct(buf.shape, buf.dtype) for buf in bufs],
        input_output_aliases={n + k: k for k in range(n)},
        compiler_params=_params("parallel", "parallel", "parallel"),
        name="store_kv_rows",
    )(*([proj] * n), *bufs)


def _dot3(x, w):
    xh, xl = _split(x)
    wh, wl = _split(w)
    m = x.shape[0]
    a = jnp.dot(jnp.concatenate([xh, xl], axis=0), wh, preferred_element_type=F32)
    return a[:m] + a[m:] + jnp.dot(xh, wl, preferred_element_type=F32)


def _dot3_nt(x, k):
    xh, xl = _split(x)
    kh, kl = _split(k)
    m = x.shape[0]
    a = _dot_nt(jnp.concatenate([xh, xl], axis=0), kh)
    return a[:m] + a[m:] + _dot_nt(xh, kl)


def _proj3_kernel(x_ref, w_ref, o_ref):
    o_ref[...] = _dot3(x_ref[...], w_ref[...])


def _proj3(x, w_stack, layer, tn):
    m, k = x.shape
    n = w_stack.shape[2]
    tn = min(tn, n)
    return pl.pallas_call(
        _proj3_kernel,
        grid=(n // tn,),
        in_specs=[pl.BlockSpec((m, k), lambda j: (0, 0)),
                  pl.BlockSpec((None, k, tn), lambda j: (layer, 0, j))],
        out_specs=pl.BlockSpec((m, tn), lambda j: (0, j)),
        out_shape=jax.ShapeDtypeStruct((m, n), F32),
        compiler_params=_params("parallel"),
        name="sample_proj",
    )(x, w_stack)


def _out_ln3_kernel(x_ref, w_ref, r_ref, g_ref, b_ref, o_ref, acc_ref, *, alpha):
    k = pl.program_id(0)

    @pl.when(k == 0)
    def _():
        acc_ref[...] = jnp.zeros_like(acc_ref)

    acc_ref[...] += _dot3(x_ref[...], w_ref[...])

    @pl.when(k == pl.num_programs(0) - 1)
    def _():
        o_ref[...] = _layer_norm(alpha * r_ref[...] + acc_ref[...], g_ref[...], b_ref[...])


def _out_ln3(x, w_stack, res, g, b, layer, alpha, tk):
    m, kdim = x.shape
    n = w_stack.shape[2]
    tk = min(tk, kdim)
    return pl.pallas_call(
        functools.partial(_out_ln3_kernel, alpha=alpha),
        grid=(kdim // tk,),
        in_specs=[pl.BlockSpec((m, tk), lambda k: (0, k)),
                  pl.BlockSpec((None, tk, n), lambda k: (layer, k, 0)),
                  pl.BlockSpec((m, n), lambda k: (0, 0)),
                  pl.BlockSpec((None, 1, n), lambda k: (layer, 0, 0)),
                  pl.BlockSpec((None, 1, n), lambda k: (layer, 0, 0))],
        out_specs=pl.BlockSpec((m, n), lambda k: (0, 0)),
        out_shape=jax.ShapeDtypeStruct((m, n), F32),
        scratch_shapes=[pltpu.VMEM((m, n), F32)],
        compiler_params=_params("arbitrary"),
        name="sample_out_ln",
    )(x, w_stack, res, g, b)


def _mlp3_kernel(x_ref, wu_ref, wd_ref, g_ref, b_ref, o_ref, acc_ref, *, alpha):
    f = pl.program_id(0)

    @pl.when(f == 0)
    def _():
        acc_ref[...] = jnp.zeros_like(acc_ref)

    h = jnp.maximum(_dot3(x_ref[...], wu_ref[...]), 0.0)
    acc_ref[...] += _dot3(h * h, wd_ref[...])

    @pl.when(f == pl.num_programs(0) - 1)
    def _():
        o_ref[...] = _layer_norm(alpha * x_ref[...] + acc_ref[...], g_ref[...], b_ref[...])


def _mlp3(x, wu_stack, wd_stack, g, b, layer, alpha, tf):
    m, d = x.shape
    ff = wu_stack.shape[2]
    tf = min(tf, ff)
    return pl.pallas_call(
        functools.partial(_mlp3_kernel, alpha=alpha),
        grid=(ff // tf,),
        in_specs=[pl.BlockSpec((m, d), lambda f: (0, 0)),
                  pl.BlockSpec((None, d, tf), lambda f: (layer, 0, f)),
                  pl.BlockSpec((None, tf, d), lambda f: (layer, f, 0)),
                  pl.BlockSpec((None, 1, d), lambda f: (layer, 0, 0)),
                  pl.BlockSpec((None, 1, d), lambda f: (layer, 0, 0))],
        out_specs=pl.BlockSpec((m, d), lambda f: (0, 0)),
        out_shape=jax.ShapeDtypeStruct((m, d), F32),
        scratch_shapes=[pltpu.VMEM((m, d), F32)],
        compiler_params=_params("arbitrary"),
        name="sample_mlp",
    )(x, wu_stack, wd_stack, g, b)


def _pick_rows(parts):
    row = lax.broadcasted_iota(jnp.int32, parts[0].shape, 0)
    out = jnp.zeros_like(parts[0])
    for h, part in enumerate(parts):
        out = jnp.where(row == h, part, out)
    return out


def _own_head_blocks(x, n_heads):
    return _pick_rows([x[:, h * HEAD_DIM:(h + 1) * HEAD_DIM] for h in range(n_heads)])


def _head_scores(q, keys):
    row = lax.broadcasted_iota(jnp.int32, q.shape, 0)
    q_bd = jnp.concatenate([jnp.where(row == h, q, 0.0) for h in range(len(keys))], axis=1)
    return _dot3_nt(q_bd, jnp.concatenate(keys, axis=1)) * ATTN_SCALE


def _head_values(w, values):
    return _dot3(w, jnp.concatenate(values, axis=1))


def _suffix_sums3(lk, uu):
    hi, lo = _split(lk)
    lo2 = (lk - hi.astype(F32) - lo.astype(F32)).astype(BF16)
    loc = jnp.dot(jnp.concatenate([hi, lo], axis=1), uu, preferred_element_type=F32)
    loc = loc + jnp.dot(lo2, uu[:SUFFIX_CHUNK], preferred_element_type=F32)
    return loc, loc[:, 0:1] + lk[:, 0:1]


def _sb_decode_kernel(pt_ref, q_ref, *refs, n_heads, n_sub):
    k_refs, v_refs = refs[:n_sub], refs[n_sub:2 * n_sub]
    uu_ref, g_ref, o_ref, acc_ref, carry_ref = refs[2 * n_sub:]
    j = pl.program_id(1)

    @pl.when(j == 0)
    def _():
        acc_ref[...] = jnp.zeros_like(acc_ref)
        carry_ref[...] = jnp.zeros_like(carry_ref)

    def tokens(page_refs, h):
        return jnp.concatenate([r[h] for r in page_refs], axis=0)

    z = _head_scores(q_ref[...], [tokens(k_refs, h) for h in range(n_heads)])
    nsp = _neg_softplus(z)
    uu = uu_ref[...]
    run = carry_ref[...]
    n_chunks = z.shape[1] // SUFFIX_CHUNK
    between = [None] * n_chunks
    for c in reversed(range(n_chunks)):
        loc, tot = _suffix_sums3(nsp[:, c * SUFFIX_CHUNK:(c + 1) * SUFFIX_CHUNK], uu)
        between[c] = loc + run
        run = run + tot
    w = jnp.exp(z + nsp + jnp.concatenate(between, axis=1))
    acc_ref[...] += _head_values(w, [tokens(v_refs, h) for h in range(n_heads)])
    carry_ref[...] = run

    @pl.when(j == pl.num_programs(1) - 1)
    def _():
        o_ref[...] = _head_rms(_own_head_blocks(acc_ref[...], n_heads), g_ref[...])


def _moba_gate_kernel(pt_ref, q_ref, *refs, n_top, n_sub, pages_per_block):
    k_refs = refs[:n_sub * pages_per_block]
    idx_ref, gate_ref = refs[n_sub * pages_per_block:]
    n = pl.program_id(1)

    @pl.when(n == 0)
    def _():
        gate_ref[...] = jnp.full_like(gate_ref, NEG_INF)

    n_heads = k_refs[0].shape[0]
    lane = lax.broadcasted_iota(jnp.int32, (n_heads, HEAD_DIM), 1)
    gate = gate_ref[0:n_heads, :]
    for b in range(n_sub):
        pages = k_refs[b * pages_per_block:(b + 1) * pages_per_block]
        mean = sum(jnp.sum(r[...], axis=1) for r in pages) * (1.0 / MOBA_BLOCK)
        g = jnp.sum(q_ref[0:n_heads, :] * mean, axis=-1, keepdims=True)
        gate = jnp.where(lane == n * n_sub + b, g, gate)
    gate_ref[0:n_heads, :] = gate

    @pl.when(n == pl.num_programs(1) - 1)
    def _():
        gate = gate_ref[...]
        lanes = lax.broadcasted_iota(jnp.int32, gate.shape, 1)
        idx = jnp.zeros(gate.shape, jnp.int32)
        for r in range(n_top):
            best = jnp.max(gate, axis=-1, keepdims=True)
            first = jnp.min(jnp.where(gate == best, lanes, HEAD_DIM), axis=-1, keepdims=True)
            idx = jnp.where(lanes == r, first, idx)
            gate = jnp.where(lanes == first, NEG_INF, gate)
        idx_ref[...] = idx


def _moba_decode_kernel(pt_ref, top_ref, slope_ref, q_ref, kn_ref, vn_ref, g_ref, *refs, n_heads, past_len, page):
    k_refs, v_refs = refs[:n_heads], refs[n_heads:2 * n_heads]
    o_ref, acc_ref, m_ref, l_ref = refs[2 * n_heads:]
    s_id, t = pl.program_id(0), pl.program_id(1)
    pages_per_block = MOBA_BLOCK // page
    q = q_ref[...]

    @pl.when(t == 0)
    def _():
        m_ref[...] = jnp.sum(q * kn_ref[...], axis=-1, keepdims=True) * ATTN_SCALE
        l_ref[...] = jnp.ones_like(l_ref)
        acc_ref[...] = vn_ref[...]

    row = lax.broadcasted_iota(jnp.int32, (8, 1), 0)
    slope = jnp.zeros((8, 1), F32)
    dist0 = jnp.zeros((8, 1), F32)
    for h in range(n_heads):
        kpos0 = top_ref[s_id, h, t // pages_per_block] * MOBA_BLOCK + (t % pages_per_block) * page
        slope = jnp.where(row == h, slope_ref[h], slope)
        dist0 = jnp.where(row == h, (past_len - kpos0).astype(F32), dist0)
    lane = lax.broadcasted_iota(jnp.int32, (8, page), 1)
    s = _head_scores(q, [r[...] for r in k_refs]) - slope * (dist0 - lane.astype(F32))
    m_old = m_ref[...]
    m_new = jnp.maximum(m_old, jnp.max(s, axis=-1, keepdims=True))
    a = jnp.exp(m_old - m_new)
    e = jnp.exp(s - m_new)
    l_ref[...] = a * l_ref[...] + jnp.sum(e, axis=-1, keepdims=True)
    acc_ref[...] = a * acc_ref[...] + _own_head_blocks(_head_values(e, [r[...] for r in v_refs]), n_heads)
    m_ref[...] = m_new

    @pl.when(t == pl.num_programs(1) - 1)
    def _():
        o_ref[...] = _head_rms(acc_ref[...] / l_ref[...], g_ref[...])


def _mem_decode_kernel(q_ref, k_ref, v_ref, g_ref, o_ref, *, n_heads):
    s = _head_scores(q_ref[...], [k_ref[:, h, :] for h in range(n_heads)])
    e = jnp.exp(s - jnp.max(s, axis=-1, keepdims=True))
    o = _head_values(e, [v_ref[:, h, :] for h in range(n_heads)])
    o = _own_head_blocks(o, n_heads) / jnp.sum(e, axis=-1, keepdims=True)
    o_ref[...] = _head_rms(o, g_ref[...])


def _heads_to_rows(x, n_heads):
    x = x.reshape(x.shape[0], n_heads, HEAD_DIM)
    return jnp.pad(x, ((0, 0), (0, 8 - n_heads), (0, 0)))


def _pages_per_step(n, most):
    return max(d for d in range(1, most + 1) if n % d == 0)


def _sample_attention(proj, gain_l, page_table, pools, mem_k, mem_v, slopes, uu, layer, h_sb, h_mb, h_mem):
    cache_k_sb, cache_v_sb, cache_k_mb, cache_v_mb = pools
    n_seq, n_pages = page_table.shape
    page = cache_k_sb.shape[3]
    past_len = n_pages * page
    hd = HEAD_DIM
    w_sb, w_mb = h_sb * hd, h_mb * hd
    q_sb = _heads_to_rows(proj[:, 0:w_sb], h_sb)
    q_mb = _heads_to_rows(proj[:, 3 * w_sb:3 * w_sb + w_mb], h_mb)
    k_mb_new = _heads_to_rows(proj[:, 3 * w_sb + w_mb:3 * w_sb + 2 * w_mb], h_mb)
    v_mb_new = _heads_to_rows(proj[:, 3 * w_sb + 2 * w_mb:3 * w_sb + 3 * w_mb], h_mb)
    q_mem = _heads_to_rows(proj[:, 3 * w_sb + 3 * w_mb:], h_mem)
    g_sb = _heads_to_rows(gain_l[None, 0:w_sb], h_sb)[0]
    g_mb = _heads_to_rows(gain_l[None, w_sb:w_sb + w_mb], h_mb)[0]
    g_mem = _heads_to_rows(gain_l[None, w_sb + w_mb:], h_mem)[0]

    row_spec = pl.BlockSpec((None, 8, hd), lambda s, j, *_: (s, 0, 0))
    gain_spec = pl.BlockSpec((8, hd), lambda s, j, *_: (0, 0))
    row_shape = jax.ShapeDtypeStruct((n_seq, 8, hd), F32)

    def page_spec(n_heads, page_of):
        return pl.BlockSpec((None, None, n_heads, page, hd),
                            lambda s, j, pt: (layer, pt[s, page_of(j)], 0, 0, 0))

    n_sub = _pages_per_step(n_pages, 8)
    sb_pages = [page_spec(h_sb, functools.partial(lambda j, i: n_pages - n_sub * (j + 1) + i, i=i))
                for i in range(n_sub)]
    o_sb = pl.pallas_call(
        functools.partial(_sb_decode_kernel, n_heads=h_sb, n_sub=n_sub),
        grid_spec=pltpu.PrefetchScalarGridSpec(
            num_scalar_prefetch=1, grid=(n_seq, n_pages // n_sub),
            in_specs=[row_spec] + sb_pages + sb_pages
                     + [pl.BlockSpec((2 * SUFFIX_CHUNK, SUFFIX_CHUNK), lambda s, j, pt: (0, 0)), gain_spec],
            out_specs=row_spec,
            scratch_shapes=[pltpu.VMEM((8, h_sb * hd), F32), pltpu.VMEM((8, 1), F32)]),
        out_shape=row_shape,
        compiler_params=_params("parallel", "arbitrary"),
        name="sample_sb_attn",
    )(page_table, q_sb, *([cache_k_sb] * n_sub), *([cache_v_sb] * n_sub), uu, g_sb)

    pages_per_block = MOBA_BLOCK // page
    n_past_blocks = past_len // MOBA_BLOCK
    n_top = min(MOBA_TOPK, n_past_blocks)
    n_gate = _pages_per_step(n_past_blocks, 8)
    gate_pages = [page_spec(h_mb, functools.partial(lambda n, i: n_gate * pages_per_block * n + i, i=i))
                  for i in range(n_gate * pages_per_block)]
    top, _ = pl.pallas_call(
        functools.partial(_moba_gate_kernel, n_top=n_top, n_sub=n_gate, pages_per_block=pages_per_block),
        grid_spec=pltpu.PrefetchScalarGridSpec(
            num_scalar_prefetch=1, grid=(n_seq, n_past_blocks // n_gate),
            in_specs=[row_spec] + gate_pages,
            out_specs=[row_spec, row_spec]),
        out_shape=[jax.ShapeDtypeStruct((n_seq, 8, hd), jnp.int32), row_shape],
        compiler_params=_params("parallel", "arbitrary"),
        name="sample_moba_gate",
    )(page_table, q_mb, *([cache_k_mb] * (n_gate * pages_per_block)))
    top = top[:, :, :max(n_top, 1)]

    def sel_page(h):
        return pl.BlockSpec(
            (None, None, None, page, hd),
            lambda s, t, pt, tp: (layer, pt[s, tp[s, h, t // pages_per_block] * pages_per_block + t % pages_per_block],
                                  h, 0, 0))

    sel_pages = [sel_page(h) for h in range(h_mb)]
    o_mb = pl.pallas_call(
        functools.partial(_moba_decode_kernel, n_heads=h_mb, past_len=past_len, page=page),
        grid_spec=pltpu.PrefetchScalarGridSpec(
            num_scalar_prefetch=2, grid=(n_seq, n_top * pages_per_block),
            in_specs=[pl.BlockSpec(memory_space=pltpu.SMEM), row_spec, row_spec, row_spec, gain_spec]
                     + sel_pages + sel_pages,
            out_specs=row_spec,
            scratch_shapes=[pltpu.VMEM((8, hd), F32), pltpu.VMEM((8, 1), F32), pltpu.VMEM((8, 1), F32)]),
        out_shape=row_shape,
        compiler_params=_params("parallel", "arbitrary"),
        name="sample_moba_attn",
    )(page_table, top, slopes, q_mb, k_mb_new, v_mb_new, g_mb, *([cache_k_mb] * h_mb), *([cache_v_mb] * h_mb))

    mem_len = mem_k.shape[2]
    mem_spec = pl.BlockSpec((None, None, mem_len, h_mem, hd), lambda s: (layer, s, 0, 0, 0))
    o_mem = pl.pallas_call(
        functools.partial(_mem_decode_kernel, n_heads=h_mem),
        grid=(n_seq,),
        in_specs=[pl.BlockSpec((None, 8, hd), lambda s: (s, 0, 0)), mem_spec, mem_spec,
                  pl.BlockSpec((8, hd), lambda s: (0, 0))],
        out_specs=pl.BlockSpec((None, 8, hd), lambda s: (s, 0, 0)),
        out_shape=row_shape,
        compiler_params=_params("parallel"),
        name="sample_mem_attn",
    )(q_mem, mem_k, mem_v, g_mem)

    return jnp.concatenate([o_sb[:, :h_sb].reshape(n_seq, w_sb), o_mb[:, :h_mb].reshape(n_seq, w_mb),
                            o_mem[:, :h_mem].reshape(n_seq, h_mem * hd)], axis=1)


def kernel(x_prompt, x_sample, mem_prompt, cache_k_sb, cache_v_sb, cache_k_moba, cache_v_moba, cache_mem_k, cache_mem_v, page_table, w_in, w_mem_kv, out_norm_g, w_o, ln1_g, ln1_b, w_up, w_down, ln2_g, ln2_b):
    batch, seq, d_model = x_prompt.shape
    n_seq, dec_seq, _ = x_sample.shape
    depth = w_in.shape[0]
    h_sb, h_mb, h_mem = cache_k_sb.shape[3], cache_k_moba.shape[3], cache_mem_k.shape[3]
    mem_len = mem_prompt.shape[1]
    hd = HEAD_DIM
    assert dec_seq == 1 and cache_k_sb.shape[4] == hd and seq % MOBA_KEYS == 0
    assert MOBA_BLOCK % cache_k_sb.shape[2] == 0 and seq // MOBA_BLOCK <= hd
    w_mem = h_mem * hd
    alpha = (2 * depth) ** 0.25

    w_in_b, w_kv_b, w_o_b = w_in.astype(BF16), w_mem_kv.astype(BF16), w_o.astype(BF16)
    w_up_b, w_down_b = w_up.astype(BF16), w_down.astype(BF16)
    gain = out_norm_g.reshape(depth, 1, d_model)
    ln1_g3, ln1_b3 = ln1_g.reshape(depth, 1, d_model), ln1_b.reshape(depth, 1, d_model)
    ln2_g3, ln2_b3 = ln2_g.reshape(depth, 1, d_model), ln2_b.reshape(depth, 1, d_model)
    slopes = jnp.exp2(-8.0 * jnp.arange(1, h_mb + 1, dtype=F32) / h_mb)
    uu = _suffix_matrix()
    pools = tuple(jnp.transpose(c, (0, 1, 3, 2, 4)) for c in (cache_k_sb, cache_v_sb, cache_k_moba, cache_v_moba))

    hp = x_prompt.reshape(batch * seq, d_model)
    hs = x_sample.reshape(n_seq, d_model)
    mem2d = mem_prompt.reshape(batch * mem_len, d_model)
    assert h_sb == h_mb
    kv_p = [jnp.zeros((depth, batch, h_sb, seq, hd), F32) for _ in range(4)]
    kv_cols = (h_sb, 2 * h_sb, 3 * h_sb + h_mb, 3 * h_sb + 2 * h_mb)
    outs = [[] for _ in range(10)]
    for l in range(depth):
        memkv = _proj(mem2d, w_kv_b, l, 512, 512)
        proj_p = _proj(hp, w_in_b, l, 1024, 1024)
        mix_p = _prompt_attention(proj_p, memkv, gain, slopes, uu, l, batch, seq, h_sb, h_mb, h_mem, mem_len)
        hp = _out_ln(mix_p, w_o_b, hp, ln1_g3, ln1_b3, l, alpha, 1024, 512)
        hp = _mlp(hp, w_up_b, w_down_b, ln2_g3, ln2_b3, l, alpha, 512, 1024)

        proj_s = _proj3(hs, w_in, l, 512)
        mix_s = _sample_attention(proj_s, out_norm_g[l], page_table, pools, cache_mem_k, cache_mem_v,
                                  slopes, uu, l, h_sb, h_mb, h_mem)
        hs = _out_ln3(mix_s, w_o, hs, ln1_g3, ln1_b3, l, alpha, 512)
        hs = _mlp3(hs, w_up, w_down, ln2_g3, ln2_b3, l, alpha, 512)

        kv_p = _store_kv_rows(proj_p, kv_p, kv_cols, l, batch, seq, h_sb)
        for i, c in enumerate(kv_cols):
            outs[6 + i].append(proj_s[:, c * hd:(c + h_sb) * hd].reshape(n_seq, 1, h_sb, hd))
        outs[4].append(memkv[:, :w_mem].reshape(batch, mem_len, h_mem, hd))
        outs[5].append(memkv[:, w_mem:].reshape(batch, mem_len, h_mem, hd))
    return (hp.reshape(batch, seq, d_model), hs.reshape(n_seq, 1, d_model),
            *[jnp.transpose(buf, (0, 1, 3, 2, 4)) for buf in kv_p],
            *[jnp.stack(o) for o in outs[4:]])
```
